```python
import math
import jax, jax.numpy as jnp
from jax import lax
import numpy as np

D_MODEL = 1024
BATCH = 16
SEQ = 4096
DEPTH = 4

GRID_W = 64
NA_HEADS = 8
NA_HEAD_DIM = 64
NA_WIDTH = NA_HEADS * NA_HEAD_DIM
NA_WIN_R = 8
NA_WIN_C = 16
DA_HEADS = 4
DA_HEAD_DIM = 64
DA_V_DIM = 2 * DA_HEAD_DIM
DA_WIDTH = DA_HEADS * DA_V_DIM
MIX_WIDTH = NA_WIDTH + DA_WIDTH
IN_COLS = 4 * NA_WIDTH + 4 * DA_WIDTH
Q_BLOCK = 128
T5_BUCKETS = 32
T5_MAX_EXACT = 8
T5_MAX_DIST = 128
NORM_EPS = 1e-6
SUBLN_EPS = 1e-5

kernel_name = "hybrid_natten_diffattn_encoder"


def rmsnorm(x, g, eps=NORM_EPS):
    xf = x.astype(jnp.float32)
    y = xf * lax.rsqrt(jnp.mean(xf * xf, axis=-1, keepdims=True) + eps)
    return (y * g.astype(jnp.float32)).astype(x.dtype)


def t5_bucket(rel):
    n = T5_BUCKETS // 2
    ret = jnp.where(rel > 0, n, 0)
    a = jnp.abs(rel)
    small = a < T5_MAX_EXACT
    af = jnp.maximum(a, 1).astype(jnp.float32)
    large = T5_MAX_EXACT + (jnp.log(af / T5_MAX_EXACT) / math.log(T5_MAX_DIST / T5_MAX_EXACT)
                            * (n - T5_MAX_EXACT)).astype(jnp.int32)
    large = jnp.minimum(large, n - 1)
    return ret + jnp.where(small, a, large)


def neighborhood_attention(q, k, v, rpb):
    b, s, h, dh = q.shape
    rows = s // GRID_W
    wr = min(NA_WIN_R, rows)
    scale = dh ** -0.5

    def to_grid(t):
        return t.reshape(b, rows, GRID_W, h, dh).transpose(0, 3, 1, 2, 4)

    qg, kg, vg = to_grid(q), to_grid(k), to_grid(v)
    c = jnp.arange(GRID_W)
    cs = jnp.clip(c - NA_WIN_C // 2, 0, GRID_W - NA_WIN_C)
    col_idx = cs[:, None] + jnp.arange(NA_WIN_C)[None, :]
    col_bias_idx = col_idx - c[:, None] + (NA_WIN_C - 1)

    def row_step(r):
        rs = jnp.clip(r - wr // 2, 0, rows - wr)
        qr = lax.dynamic_index_in_dim(qg, r, axis=2, keepdims=False)
        kr = lax.dynamic_slice_in_dim(kg, rs, wr, axis=2)[:, :, :, col_idx]
        vr = lax.dynamic_slice_in_dim(vg, rs, wr, axis=2)[:, :, :, col_idx]
        row_bias_idx = rs + jnp.arange(wr) - r + (NA_WIN_R - 1)
        bias = rpb[:, row_bias_idx][:, :, col_bias_idx]
        bias = bias.transpose(0, 2, 1, 3).astype(jnp.float32)
        logits = jnp.einsum('bhcd,bhicjd->bhcij', qr, kr).astype(jnp.float32) * scale + bias[None]
        p = jax.nn.softmax(logits.reshape(b, h, GRID_W, wr * NA_WIN_C), axis=-1)
        p = p.reshape(b, h, GRID_W, wr, NA_WIN_C).astype(vr.dtype)
        return jnp.einsum('bhcij,bhicjd->bhcd', p, vr)

    out = lax.map(row_step, jnp.arange(rows))
    return out.transpose(1, 0, 3, 2, 4).reshape(b, s, h * dh)


def diff_attention(q, k, v, t5_table, lam, lam_init, subln_g):
    b, s, h, _, dh = q.shape
    nb = s // Q_BLOCK
    scale = dh ** -0.5
    qb = q.reshape(b, nb, Q_BLOCK, h, 2, dh).transpose(1, 0, 2, 3, 4, 5)
    kpos = jnp.arange(s)

    def block_step(args):
        qblk, i = args
        qpos = i * Q_BLOCK + jnp.arange(Q_BLOCK)
        bias = t5_table[t5_bucket(kpos[None, :] - qpos[:, None])]
        bias = bias.transpose(2, 0, 1).astype(jnp.float32)
        logits = jnp.einsum('bqhtd,bkhtd->bhtqk', qblk, k).astype(jnp.float32) * scale
        p = jax.nn.softmax(logits + bias[None, :, None], axis=-1)
        attn = (p[:, :, 0] - lam * p[:, :, 1]).astype(v.dtype)
        return jnp.einsum('bhqk,bkhe->bqhe', attn, v)

    out = lax.map(block_step, (qb, jnp.arange(nb)))
    out = out.transpose(1, 0, 2, 3, 4).reshape(b, s, h, DA_V_DIM)
    out = rmsnorm(out, subln_g, eps=SUBLN_EPS) * (1.0 - lam_init)
    return out.reshape(b, s, h * DA_V_DIM)


def setup_inputs(seed: int = 0) -> dict:
    key = jax.random.key(seed)
    ks = jax.random.split(key, 12)
    f32 = jnp.float32
    x = jax.random.normal(ks[0], (BATCH, SEQ, D_MODEL), f32)
    norm_g = 1.0 + 0.01 * jax.random.normal(ks[1], (DEPTH, D_MODEL), f32)
    w_in = jax.random.normal(ks[2], (DEPTH, D_MODEL, IN_COLS), f32) * D_MODEL ** -0.5
    na_rpb = 0.1 * jax.random.normal(ks[3], (DEPTH, NA_HEADS, 2 * NA_WIN_R - 1, 2 * NA_WIN_C - 1), f32)
    lambda_q1 = 0.1 * jax.random.normal(ks[4], (DEPTH, DA_HEAD_DIM), f32)
    lambda_k1 = 0.1 * jax.random.normal(ks[5], (DEPTH, DA_HEAD_DIM), f32)
    lambda_q2 = 0.1 * jax.random.normal(ks[6], (DEPTH, DA_HEAD_DIM), f32)
    lambda_k2 = 0.1 * jax.random.normal(ks[7], (DEPTH, DA_HEAD_DIM), f32)
    subln_g = 1.0 + 0.01 * jax.random.normal(ks[8], (DEPTH, DA_V_DIM), f32)
    t5_table = 0.1 * jax.random.normal(ks[9], (T5_BUCKETS, DA_HEADS), f32)
    w_out = jax.random.normal(ks[10], (DEPTH, MIX_WIDTH, D_MODEL), f32) * MIX_WIDTH ** -0.5
    final_g = 1.0 + 0.01 * jax.random.normal(ks[11], (D_MODEL,), f32)
    return {"x": x, "norm_g": norm_g, "w_in": w_in, "na_rpb": na_rpb,
            "lambda_q1": lambda_q1, "lambda_k1": lambda_k1,
            "lambda_q2": lambda_q2, "lambda_k2": lambda_k2,
            "subln_g": subln_g, "t5_table": t5_table,
            "w_out": w_out, "final_g": final_g}


def reference(x, norm_g, w_in, na_rpb, lambda_q1, lambda_k1, lambda_q2, lambda_k2,
              subln_g, t5_table, w_out, final_g):
    b, s, _ = x.shape
    A = NA_WIDTH
    o_b = 4 * NA_WIDTH
    Bw = DA_WIDTH
    for l in range(DEPTH):
        h = rmsnorm(x, norm_g[l])
        proj = h @ w_in[l]
        q_a = proj[..., 0:A].reshape(b, s, NA_HEADS, NA_HEAD_DIM)
        k_a = proj[..., A:2 * A].reshape(b, s, NA_HEADS, NA_HEAD_DIM)
        v_a = proj[..., 2 * A:3 * A].reshape(b, s, NA_HEADS, NA_HEAD_DIM)
        g_a = proj[..., 3 * A:4 * A]
        out_a = neighborhood_attention(q_a, k_a, v_a, na_rpb[l])
        q_b = proj[..., o_b:o_b + Bw].reshape(b, s, DA_HEADS, 2, DA_HEAD_DIM)
        k_b = proj[..., o_b + Bw:o_b + 2 * Bw].reshape(b, s, DA_HEADS, 2, DA_HEAD_DIM)
        v_b = proj[..., o_b + 2 * Bw:o_b + 3 * Bw].reshape(b, s, DA_HEADS, DA_V_DIM)
        g_b = proj[..., o_b + 3 * Bw:o_b + 4 * Bw]
        lam_init = 0.8 - 0.6 * math.exp(-0.3 * l)
        lam = (jnp.exp(jnp.sum(lambda_q1[l].astype(jnp.float32) * lambda_k1[l].astype(jnp.float32)))
               - jnp.exp(jnp.sum(lambda_q2[l].astype(jnp.float32) * lambda_k2[l].astype(jnp.float32)))
               + lam_init)
        out_b = diff_attention(q_b, k_b, v_b, t5_table, lam, lam_init, subln_g[l])
        y = jnp.concatenate([out_a * jax.nn.silu(g_a), out_b * jax.nn.silu(g_b)], axis=-1)
        x = x + y @ w_out[l]
    return rmsnorm(x, final_g)
```

```python
import functools
import math

import jax
import jax.numpy as jnp
import numpy as np
from jax import lax
from jax.experimental import pallas as pl
from jax.experimental.pallas import tpu as pltpu

F32 = jnp.float32
BF16 = jnp.bfloat16

GRID_W = 64
NA_HEADS = 8
NA_HEAD_DIM = 64
NA_WIDTH = NA_HEADS * NA_HEAD_DIM
NA_WIN_R = 8
NA_WIN_C = 16
DA_HEADS = 4
DA_HEAD_DIM = 64
DA_V_DIM = 2 * DA_HEAD_DIM
DA_WIDTH = DA_HEADS * DA_V_DIM
MIX_WIDTH = NA_WIDTH + DA_WIDTH
IN_COLS = 4 * NA_WIDTH + 4 * DA_WIDTH
T5_BUCKETS = 32
T5_MAX_EXACT = 8
T5_MAX_DIST = 128
NORM_EPS = 1e-6
SUBLN_EPS = 1e-5

LANES = 128
NEG_BIG = -1e30

PROJ_BM = 512
PROJ_BN = 1024
DA_BQ = 256
DA_BK = 512
DA_NTILES = 6
VMEM_LIMIT = 48 * 1024 * 1024

_QA, _KA, _VA = 0, NA_WIDTH // LANES, 2 * NA_WIDTH // LANES
_QB = 4 * NA_WIDTH // LANES
_KB = _QB + DA_WIDTH // LANES
_VB = _KB + DA_WIDTH // LANES


def _inproj_kernel(x_ref, g_ref, w_ref, o_ref):
    x = x_ref[...]
    ms = jnp.mean(x * x, axis=-1, keepdims=True)
    h = (x * lax.rsqrt(ms + NORM_EPS) * g_ref[...]).astype(BF16)
    for n in range(IN_COLS // PROJ_BN):
        cols = slice(n * PROJ_BN, (n + 1) * PROJ_BN)
        o_ref[:, cols] = jnp.dot(h, w_ref[:, cols], preferred_element_type=F32).astype(BF16)


def _inproj(x2, g, w):
    m, d = x2.shape
    return pl.pallas_call(
        _inproj_kernel,
        grid=(m // PROJ_BM,),
        in_specs=[
            pl.BlockSpec((PROJ_BM, d), lambda i: (i, 0)),
            pl.BlockSpec((1, d), lambda i: (0, 0)),
            pl.BlockSpec((d, IN_COLS), lambda i: (0, 0)),
        ],
        out_specs=pl.BlockSpec((PROJ_BM, IN_COLS), lambda i: (i, 0)),
        out_shape=jax.ShapeDtypeStruct((m, IN_COLS), BF16),
        compiler_params=pltpu.CompilerParams(
            dimension_semantics=("arbitrary",), vmem_limit_bytes=VMEM_LIMIT),
        name="inproj",
    )(x2, g, w)


def _na_kernel(q_ref, k_ref, v_ref, bias_ref, o_ref, *, rows):
    wr = min(NA_WIN_R, rows)
    nkeys = wr * GRID_W
    lo = lax.broadcasted_iota(jnp.int32, (1, LANES), 1) < NA_HEAD_DIM
    scale = NA_HEAD_DIM ** -0.5

    def row(r, carry):
        rs = jnp.clip(r - wr // 2, 0, rows - wr)
        q = q_ref[0, pl.ds(pl.multiple_of(r * GRID_W, GRID_W), GRID_W), :] * scale
        zero = jnp.zeros_like(q)
        qq = jnp.concatenate([jnp.where(lo, q, zero), jnp.where(lo, zero, q)], axis=0)
        kstart = pl.multiple_of(rs * GRID_W, GRID_W)
        kw = k_ref[0, pl.ds(kstart, nkeys), :]
        s = lax.dot_general(qq, kw, (((1,), (1,)), ((), ())), preferred_element_type=F32)
        s = s + bias_ref[0, r - rs]
        m = jnp.max(s, axis=-1, keepdims=True)
        p = jnp.exp(s - m)
        l = jnp.sum(p, axis=-1, keepdims=True)
        pb = p.astype(BF16)
        vw = v_ref[0, pl.ds(kstart, nkeys), :]
        vzero = jnp.zeros_like(vw)
        o0 = jnp.dot(pb[:GRID_W], jnp.where(lo, vw, vzero), preferred_element_type=F32)
        o1 = jnp.dot(pb[GRID_W:], jnp.where(lo, vzero, vw), preferred_element_type=F32)
        o = o0 / l[:GRID_W] + o1 / l[GRID_W:]
        o_ref[0, pl.ds(pl.multiple_of(r * GRID_W, GRID_W), GRID_W), :] = o.astype(BF16)
        return carry

    lax.fori_loop(0, rows, row, 0)


def _na_attention(proj, na_bias):
    b, s, _ = proj.shape
    rows = s // GRID_W
    npairs = NA_HEADS // 2
    wr = min(NA_WIN_R, rows)
    blk = lambda off: pl.BlockSpec((1, s, LANES), lambda hp, bi: (bi, 0, off + hp))
    return pl.pallas_call(
        functools.partial(_na_kernel, rows=rows),
        grid=(npairs, b),
        in_specs=[
            blk(_QA), blk(_KA), blk(_VA),
            pl.BlockSpec((1, wr, 2 * GRID_W, wr * GRID_W), lambda hp, bi: (hp, 0, 0, 0)),
        ],
        out_specs=pl.BlockSpec((1, s, LANES), lambda hp, bi: (bi, 0, hp)),
        out_shape=jax.ShapeDtypeStruct((b, s, NA_WIDTH), BF16),
        compiler_params=pltpu.CompilerParams(
            dimension_semantics=("arbitrary", "arbitrary"), vmem_limit_bytes=VMEM_LIMIT),
        name="na_attn",
    )(proj, proj, proj, na_bias)


def _na_bias_tables(na_rpb, rows):
    wr = min(NA_WIN_R, rows)
    d = np.arange(wr)[:, None]
    i = np.arange(wr)[None, :]
    ri = np.clip(i - d + (NA_WIN_R - 1), 0, 2 * NA_WIN_R - 2)
    c = np.arange(GRID_W)[:, None]
    j = np.arange(GRID_W)[None, :]
    cs = np.clip(c - NA_WIN_C // 2, 0, GRID_W - NA_WIN_C)
    inside = (j >= cs) & (j < cs + NA_WIN_C)
    ci = np.clip(j - c + (NA_WIN_C - 1), 0, 2 * NA_WIN_C - 2)
    t = na_rpb[:, :, ri]
    t = t[..., ci]
    t = jnp.where(inside[None, None, None, None], t, NEG_BIG)
    nl = na_rpb.shape[0]
    t = t.transpose(0, 1, 2, 4, 3, 5)
    t = t.reshape(nl, NA_HEADS // 2, 2, wr, GRID_W, wr * GRID_W)
    t = t.transpose(0, 1, 3, 2, 4, 5)
    return t.reshape(nl, NA_HEADS // 2, wr, 2 * GRID_W, wr * GRID_W).astype(F32)


def _da_kernel(lamv_ref, q_ref, k_ref, v_ref, bias_ref, g_ref, o_ref, vt_scr, s_scr, *, seq, lam_init):
    nchunks = seq // DA_BK
    nq = seq // DA_BQ
    lo = lax.broadcasted_iota(jnp.int32, (1, LANES), 1) < DA_HEAD_DIM
    scale = DA_HEAD_DIM ** -0.5

    lv = lamv_ref[...]
    lam = (jnp.exp(jnp.sum(lv[0:1] * lv[1:2], axis=-1, keepdims=True))
           - jnp.exp(jnp.sum(lv[2:3] * lv[3:4], axis=-1, keepdims=True)) + lam_init)

    for c in range(nchunks):
        vt_scr[c] = v_ref[0, c * DA_BK:(c + 1) * DA_BK, :].T

    def qblock(qi, carry):
        qstart = pl.multiple_of(qi * DA_BQ, DA_BQ)
        q = q_ref[0, pl.ds(qstart, DA_BQ), :] * scale
        zero = jnp.zeros_like(q)
        qstat = jnp.concatenate([jnp.where(lo, q, zero), jnp.where(lo, zero, q)], axis=0)

        def pass1(c, m8):
            kc = k_ref[0, pl.ds(pl.multiple_of(c * DA_BK, DA_BK), DA_BK), :]
            s = lax.dot_general(kc, qstat, (((1,), (1,)), ((), ())), preferred_element_type=F32)
            t = jnp.clip((DA_BK // DA_BQ) * c - qi + 3, 0, DA_NTILES - 1)
            bt = bias_ref[0, t]
            s = s + jnp.concatenate([bt, bt], axis=1)
            s_scr[c] = s
            return jnp.maximum(m8, jnp.max(s.reshape(DA_BK // 8, 8, 2 * DA_BQ), axis=0))

        m8 = lax.fori_loop(0, nchunks, pass1, jnp.full((8, 2 * DA_BQ), -jnp.inf, F32))
        m = jnp.max(m8, axis=0, keepdims=True)

        def pass2(c, carry2):
            l8, a1, a2 = carry2
            p = jnp.exp(s_scr[c] - m)
            l8 = l8 + jnp.sum(p.reshape(DA_BK // 8, 8, 2 * DA_BQ), axis=0)
            pb = p.astype(BF16)
            vt = vt_scr[c]
            a1 = a1 + jnp.dot(vt, pb[:, :DA_BQ], preferred_element_type=F32)
            a2 = a2 + jnp.dot(vt, pb[:, DA_BQ:], preferred_element_type=F32)
            return l8, a1, a2

        zacc = jnp.zeros((DA_V_DIM, DA_BQ), F32)
        l8, a1, a2 = lax.fori_loop(0, nchunks, pass2, (jnp.zeros((8, 2 * DA_BQ), F32), zacc, zacc))
        l = jnp.sum(l8, axis=0, keepdims=True)
        o = a1 / l[:, :DA_BQ] - lam * (a2 / l[:, DA_BQ:])
        ms = jnp.mean(o * o, axis=0, keepdims=True)
        y = o * lax.rsqrt(ms + SUBLN_EPS) * g_ref[...]
        y = y * (1.0 - lam_init)
        o_ref[0, pl.ds(qstart, DA_BQ), :] = y.T.astype(BF16)
        return carry

    lax.fori_loop(0, nq, qblock, 0)


def _da_attention(proj, lamv, bias_tiles, subln_col, lam_init):
    b, s, _ = proj.shape
    blk = lambda off: pl.BlockSpec((1, s, LANES), lambda h, bi: (bi, 0, off + h))
    return pl.pallas_call(
        functools.partial(_da_kernel, seq=s, lam_init=lam_init),
        grid=(DA_HEADS, b),
        in_specs=[
            pl.BlockSpec((4, DA_HEAD_DIM), lambda h, bi: (0, 0)),
            blk(_QB), blk(_KB), blk(_VB),
            pl.BlockSpec((1, DA_NTILES, DA_BK, DA_BQ), lambda h, bi: (h, 0, 0, 0)),
            pl.BlockSpec((DA_V_DIM, 1), lambda h, bi: (0, 0)),
        ],
        out_specs=pl.BlockSpec((1, s, LANES), lambda h, bi: (bi, 0, h)),
        out_shape=jax.ShapeDtypeStruct((b, s, DA_WIDTH), BF16),
        scratch_shapes=[
            pltpu.VMEM((s // DA_BK, DA_V_DIM, DA_BK), BF16),
            pltpu.VMEM((s // DA_BK, DA_BK, 2 * DA_BQ), F32),
        ],
        compiler_params=pltpu.CompilerParams(
            dimension_semantics=("arbitrary", "arbitrary"), vmem_limit_bytes=VMEM_LIMIT),
        name="da_attn",
    )(lamv, proj, proj, proj, bias_tiles, subln_col)


def _t5_bucket(rel):
    n = T5_BUCKETS // 2
    ret = jnp.where(rel > 0, n, 0)
    a = jnp.abs(rel)
    small = a < T5_MAX_EXACT
    af = jnp.maximum(a, 1).astype(jnp.float32)
    large = T5_MAX_EXACT + (jnp.log(af / T5_MAX_EXACT) / math.log(T5_MAX_DIST / T5_MAX_EXACT)
                            * (n - T5_MAX_EXACT)).astype(jnp.int32)
    large = jnp.minimum(large, n - 1)
    return ret + jnp.where(small, a, large)


def _da_bias_tiles(t5_table):
    t = jnp.arange(DA_NTILES)[:, None, None]
    i = jnp.arange(DA_BK)[None, :, None]
    j = jnp.arange(DA_BQ)[None, None, :]
    rel = DA_BQ * (t - 3) + i - j
    return t5_table[_t5_bucket(rel)].transpose(3, 0, 1, 2).astype(F32)


def _silu(g):
    return g / (1.0 + jnp.exp(-g))


def _outproj_kernel(oa_ref, ob_ref, ga_ref, gb_ref, x_ref, w_ref, fg_ref, o_ref, *, final):
    ya = (oa_ref[...].astype(F32) * _silu(ga_ref[...].astype(F32))).astype(BF16)
    yb = (ob_ref[...].astype(F32) * _silu(gb_ref[...].astype(F32))).astype(BF16)
    acc = jnp.dot(ya, w_ref[:NA_WIDTH, :], preferred_element_type=F32)
    acc = acc + jnp.dot(yb, w_ref[NA_WIDTH:, :], preferred_element_type=F32)
    xn = x_ref[...] + acc
    if final:
        ms = jnp.mean(xn * xn, axis=-1, keepdims=True)
        xn = xn * lax.rsqrt(ms + NORM_EPS) * fg_ref[...]
    o_ref[...] = xn


def _outproj(out_a, out_b, proj2, x2, w, final_g, final):
    m, d = x2.shape
    ga_blk = 3 * NA_WIDTH // NA_WIDTH
    gb_blk = (4 * NA_WIDTH + 3 * DA_WIDTH) // DA_WIDTH
    return pl.pallas_call(
        functools.partial(_outproj_kernel, final=final),
        grid=(m // PROJ_BM,),
        in_specs=[
            pl.BlockSpec((PROJ_BM, NA_WIDTH), lambda i: (i, 0)),
            pl.BlockSpec((PROJ_BM, DA_WIDTH), lambda i: (i, 0)),
            pl.BlockSpec((PROJ_BM, NA_WIDTH), lambda i: (i, ga_blk)),
            pl.BlockSpec((PROJ_BM, DA_WIDTH), lambda i: (i, gb_blk)),
            pl.BlockSpec((PROJ_BM, d), lambda i: (i, 0)),
            pl.BlockSpec((MIX_WIDTH, d), lambda i: (0, 0)),
            pl.BlockSpec((1, d), lambda i: (0, 0)),
        ],
        out_specs=pl.BlockSpec((PROJ_BM, d), lambda i: (i, 0)),
        out_shape=jax.ShapeDtypeStruct((m, d), F32),
        input_output_aliases={4: 0},
        compiler_params=pltpu.CompilerParams(
            dimension_semantics=("arbitrary",), vmem_limit_bytes=VMEM_LIMIT),
        name="outproj",
    )(out_a, out_b, proj2, proj2, x2, w, final_g)


def kernel(x, norm_g, w_in, na_rpb, lambda_q1, lambda_k1, lambda_q2, lambda_k2, subln_g, t5_table, w_out, final_g):
    b, s, d = x.shape
    depth = w_in.shape[0]
    assert s % DA_BK == 0 and s % GRID_W == 0 and (b * s) % PROJ_BM == 0
    rows = s // GRID_W

    w_in_b = w_in.astype(BF16)
    w_out_b = w_out.astype(BF16)
    na_bias = _na_bias_tables(na_rpb.astype(F32), rows)
    da_bias = _da_bias_tiles(t5_table)
    lamv = jnp.stack([lambda_q1, lambda_k1, lambda_q2, lambda_k2], axis=1).astype(F32)
    fg = final_g.reshape(1, d).astype(F32)

    x2 = x.reshape(b * s, d)
    for l in range(depth):
        lam_init = 0.8 - 0.6 * math.exp(-0.3 * l)
        proj2 = _inproj(x2, norm_g[l].reshape(1, d).astype(F32), w_in_b[l])
        proj = proj2.reshape(b, s, IN_COLS)
        out_a = _na_attention(proj, na_bias[l])
        out_b = _da_attention(proj, lamv[l], da_bias, subln_g[l].reshape(DA_V_DIM, 1).astype(F32), lam_init)
        x2 = _outproj(out_a.reshape(b * s, NA_WIDTH), out_b.reshape(b * s, DA_WIDTH), proj2, x2,
                      w_out_b[l], fg, final=(l == depth - 1))
    return x2.reshape(b, s, d)
```

```python
import functools
import math

import jax
import jax.numpy as jnp
import numpy as np
from jax import lax
from jax.experimental import pallas as pl
from jax.experimental.pallas import tpu as pltpu

F32 = jnp.float32
BF16 = jnp.bfloat16

GRID_W = 64
NA_HEADS = 8
NA_HEAD_DIM = 64
NA_WIDTH = NA_HEADS * NA_HEAD_DIM
NA_WIN_R = 8
NA_WIN_C = 16
DA_HEADS = 4
DA_HEAD_DIM = 64
DA_V_DIM = 2 * DA_HEAD_DIM
DA_WIDTH = DA_HEADS * DA_V_DIM
MIX_WIDTH = NA_WIDTH + DA_WIDTH
IN_COLS = 4 * NA_WIDTH + 4 * DA_WIDTH
T5_BUCKETS = 32
T5_MAX_EXACT = 8
T5_MAX_DIST = 128
NORM_EPS = 1e-6
SUBLN_EPS = 1e-5

LANES = 128
SUBLANES = 8
NEG_BIG = -1e30
LOG2E = math.log2(math.e)

PROJ_BM = 512
PROJ_BN = 1024
NA_UNROLL = 4
DA_BQ = 256
DA_BK = 512
DA_NTILES = 6
VMEM_LIMIT = 48 * 1024 * 1024

_QA, _KA, _VA = 0, NA_WIDTH // LANES, 2 * NA_WIDTH // LANES
_QB = 4 * NA_WIDTH // LANES
_KB = _QB + DA_WIDTH // LANES
_VB = _KB + DA_WIDTH // LANES


def _inproj_kernel(x_ref, g_ref, w_ref, o_ref):
    x = x_ref[...]
    ms = jnp.mean(x * x, axis=-1, keepdims=True)
    h = (x * lax.rsqrt(ms + NORM_EPS) * g_ref[...]).astype(BF16)
    for n in range(IN_COLS // PROJ_BN):
        cols = slice(n * PROJ_BN, (n + 1) * PROJ_BN)
        o_ref[:, cols] = jnp.dot(h, w_ref[:, cols], preferred_element_type=F32).astype(BF16)


def _inproj(x2, g, w):
    m, d = x2.shape
    return pl.pallas_call(
        _inproj_kernel,
        grid=(m // PROJ_BM,),
        in_specs=[
            pl.BlockSpec((PROJ_BM, d), lambda i: (i, 0)),
            pl.BlockSpec((1, d), lambda i: (0, 0)),
            pl.BlockSpec((d, IN_COLS), lambda i: (0, 0)),
        ],
        out_specs=pl.BlockSpec((PROJ_BM, IN_COLS), lambda i: (i, 0)),
        out_shape=jax.ShapeDtypeStruct((m, IN_COLS), BF16),
        compiler_params=pltpu.CompilerParams(
            dimension_semantics=("arbitrary",), vmem_limit_bytes=VMEM_LIMIT),
        name="inproj",
    )(x2, g, w)


def _fold_logit_scale(w_in):
    col = np.ones((IN_COLS,), np.float32)
    col[0:NA_WIDTH] = NA_HEAD_DIM ** -0.5 * LOG2E
    col[4 * NA_WIDTH:4 * NA_WIDTH + DA_WIDTH] = DA_HEAD_DIM ** -0.5 * LOG2E
    return (w_in * col).astype(BF16)


def _na_kernel(q_ref, k_ref, v_ref, bias_ref, o_ref, *, rows):
    wr = min(NA_WIN_R, rows)
    nkeys = wr * GRID_W
    lo = lax.broadcasted_iota(jnp.int32, (1, LANES), 1) < NA_HEAD_DIM

    def row(r, carry):
        rs = jnp.clip(r - wr // 2, 0, rows - wr)
        q = q_ref[0, pl.ds(pl.multiple_of(r * GRID_W, GRID_W), GRID_W), :]
        zero = jnp.zeros_like(q)
        qq = jnp.concatenate([jnp.where(lo, q, zero), jnp.where(lo, zero, q)], axis=0)
        kstart = pl.multiple_of(rs * GRID_W, GRID_W)
        kw = k_ref[0, pl.ds(kstart, nkeys), :]
        s = lax.dot_general(qq, kw, (((1,), (1,)), ((), ())), preferred_element_type=F32)
        s = s + bias_ref[0, r - rs]
        m = jnp.max(s, axis=-1, keepdims=True)
        p = jnp.exp2(s - m)
        l = jnp.sum(p, axis=-1, keepdims=True)
        pb = p.astype(BF16)
        vw = v_ref[0, pl.ds(kstart, nkeys), :]
        vzero = jnp.zeros_like(vw)
        o0 = jnp.dot(pb[:GRID_W], jnp.where(lo, vw, vzero), preferred_element_type=F32)
        o1 = jnp.dot(pb[GRID_W:], jnp.where(lo, vzero, vw), preferred_element_type=F32)
        o = o0 / l[:GRID_W] + o1 / l[GRID_W:]
        o_ref[0, pl.ds(pl.multiple_of(r * GRID_W, GRID_W), GRID_W), :] = o.astype(BF16)
        return carry

    lax.fori_loop(0, rows, row, 0, unroll=math.gcd(NA_UNROLL, rows))


def _na_attention(proj, na_bias):
    b, s, _ = proj.shape
    rows = s // GRID_W
    npairs = NA_HEADS // 2
    wr = min(NA_WIN_R, rows)
    blk = lambda off: pl.BlockSpec((1, s, LANES), lambda hp, bi: (bi, 0, off + hp))
    return pl.pallas_call(
        functools.partial(_na_kernel, rows=rows),
        grid=(npairs, b),
        in_specs=[
            blk(_QA), blk(_KA), blk(_VA),
            pl.BlockSpec((1, wr, 2 * GRID_W, wr * GRID_W), lambda hp, bi: (hp, 0, 0, 0)),
        ],
        out_specs=pl.BlockSpec((1, s, LANES), lambda hp, bi: (bi, 0, hp)),
        out_shape=jax.ShapeDtypeStruct((b, s, NA_WIDTH), BF16),
        compiler_params=pltpu.CompilerParams(
            dimension_semantics=("arbitrary", "arbitrary"), vmem_limit_bytes=VMEM_LIMIT),
        name="na_attn",
    )(proj, proj, proj, na_bias)


def _na_bias_tables(na_rpb, rows):
    wr = min(NA_WIN_R, rows)
    nl = na_rpb.shape[0]
    d = np.arange(wr)[:, None]
    i = np.arange(wr)[None, :]
    row_sel = (i - d + (NA_WIN_R - 1))[:, :, None] == np.arange(2 * NA_WIN_R - 1)
    c = np.arange(GRID_W)[:, None]
    j = np.arange(GRID_W)[None, :]
    cs = np.clip(c - NA_WIN_C // 2, 0, GRID_W - NA_WIN_C)
    inside = (j >= cs) & (j < cs + NA_WIN_C)
    col_sel = ((j - c + (NA_WIN_C - 1))[:, :, None] == np.arange(2 * NA_WIN_C - 1)) & inside[:, :, None]
    hi = lax.Precision.HIGHEST
    rpb = (na_rpb * LOG2E).reshape(nl, NA_HEADS // 2, 2, 2 * NA_WIN_R - 1, 2 * NA_WIN_C - 1)
    t = jnp.einsum("lpeab,cjb->lpeacj", rpb, col_sel.astype(np.float32), precision=hi)
    t = jnp.einsum("dia,lpeacj->lpdecij", row_sel.astype(np.float32), t, precision=hi)
    t = t + np.where(inside, 0.0, NEG_BIG).astype(np.float32)[:, None, :]
    return t.reshape(nl, NA_HEADS // 2, wr, 2 * GRID_W, wr * GRID_W)


def _da_kernel(lamv_ref, q_ref, k_ref, v_ref, bias_ref, g_ref, o_ref, vt_scr, s_even, s_odd, *, seq, lam_init):
    nchunks = seq // DA_BK
    nq = seq // DA_BQ
    lo = lax.broadcasted_iota(jnp.int32, (1, LANES), 1) < DA_HEAD_DIM
    nsub = DA_BK // SUBLANES

    lv = lamv_ref[...]
    lam = (jnp.exp(jnp.sum(lv[0:1] * lv[1:2], axis=-1, keepdims=True))
           - jnp.exp(jnp.sum(lv[2:3] * lv[3:4], axis=-1, keepdims=True)) + lam_init)

    for c in range(nchunks):
        vt_scr[c] = v_ref[0, c * DA_BK:(c + 1) * DA_BK, :].T

    def logits(qi, s_scr):
        q = q_ref[0, pl.ds(pl.multiple_of(qi * DA_BQ, DA_BQ), DA_BQ), :]
        zero = jnp.zeros_like(q)
        q_t = jnp.concatenate([jnp.where(lo, q, zero), jnp.where(lo, zero, q)], axis=0).T
        m8 = None
        for c in range(nchunks):
            s = jnp.dot(k_ref[0, c * DA_BK:(c + 1) * DA_BK, :], q_t, preferred_element_type=F32)
            t = jnp.clip((DA_BK // DA_BQ) * c - qi + 3, 0, DA_NTILES - 1)
            bt = bias_ref[0, t]
            s = s + jnp.concatenate([bt, bt], axis=1)
            s_scr[c] = s
            cm = jnp.max(s.reshape(nsub, SUBLANES, 2 * DA_BQ), axis=0)
            m8 = cm if m8 is None else jnp.maximum(m8, cm)
        return jnp.max(m8, axis=0, keepdims=True)

    def attend(qi, s_scr, m):
        l8 = jnp.zeros((SUBLANES, 2 * DA_BQ), F32)
        a1 = jnp.zeros((DA_V_DIM, DA_BQ), F32)
        a2 = jnp.zeros((DA_V_DIM, DA_BQ), F32)
        for c in range(nchunks):
            p = jnp.exp2(s_scr[c] - m)
            l8 = l8 + jnp.sum(p.reshape(nsub, SUBLANES, 2 * DA_BQ), axis=0)
            pb = p.astype(BF16)
            vt = vt_scr[c]
            a1 = a1 + jnp.dot(vt, pb[:, :DA_BQ], preferred_element_type=F32)
            a2 = a2 + jnp.dot(vt, pb[:, DA_BQ:], preferred_element_type=F32)
        l = jnp.sum(l8, axis=0, keepdims=True)
        o = a1 / l[:, :DA_BQ] - lam * (a2 / l[:, DA_BQ:])
        ms = jnp.mean(o * o, axis=0, keepdims=True)
        y = o * lax.rsqrt(ms + SUBLN_EPS) * g_ref[...]
        y = y * (1.0 - lam_init)
        o_ref[0, pl.ds(pl.multiple_of(qi * DA_BQ, DA_BQ), DA_BQ), :] = y.T.astype(BF16)

    m_even = logits(0, s_even)

    def pair(jj, m_even):
        qi = 2 * jj
        m_odd = logits(qi + 1, s_odd)
        attend(qi, s_even, m_even)
        m_next = logits(qi + 2, s_even)
        attend(qi + 1, s_odd, m_odd)
        return m_next

    m_even = lax.fori_loop(0, nq // 2 - 1, pair, m_even)
    m_odd = logits(nq - 1, s_odd)
    attend(nq - 2, s_even, m_even)
    attend(nq - 1, s_odd, m_odd)


def _da_attention(proj, lamv, bias_tiles, subln_col, lam_init):
    b, s, _ = proj.shape
    assert (s // DA_BQ) % 2 == 0
    blk = lambda off: pl.BlockSpec((1, s, LANES), lambda h, bi: (bi, 0, off + h))
    logits_scratch = pltpu.VMEM((s // DA_BK, DA_BK, 2 * DA_BQ), F32)
    return pl.pallas_call(
        functools.partial(_da_kernel, seq=s, lam_init=lam_init),
        grid=(DA_HEADS, b),
        in_specs=[
            pl.BlockSpec((4, DA_HEAD_DIM), lambda h, bi: (0, 0)),
            blk(_QB), blk(_KB), blk(_VB),
            pl.BlockSpec((1, DA_NTILES, DA_BK, DA_BQ), lambda h, bi: (h, 0, 0, 0)),
            pl.BlockSpec((DA_V_DIM, 1), lambda h, bi: (0, 0)),
        ],
        out_specs=pl.BlockSpec((1, s, LANES), lambda h, bi: (bi, 0, h)),
        out_shape=jax.ShapeDtypeStruct((b, s, DA_WIDTH), BF16),
        scratch_shapes=[
            pltpu.VMEM((s // DA_BK, DA_V_DIM, DA_BK), BF16),
            logits_scratch,
            logits_scratch,
        ],
        compiler_params=pltpu.CompilerParams(
            dimension_semantics=("arbitrary", "arbitrary"), vmem_limit_bytes=VMEM_LIMIT),
        name="da_attn",
    )(lamv, proj, proj, proj, bias_tiles, subln_col)


def _t5_bucket(rel):
    n = T5_BUCKETS // 2
    ret = jnp.where(rel > 0, n, 0)
    a = jnp.abs(rel)
    small = a < T5_MAX_EXACT
    af = jnp.maximum(a, 1).astype(jnp.float32)
    large = T5_MAX_EXACT + (jnp.log(af / T5_MAX_EXACT) / math.log(T5_MAX_DIST / T5_MAX_EXACT)
                            * (n - T5_MAX_EXACT)).astype(jnp.int32)
    large = jnp.minimum(large, n - 1)
    return ret + jnp.where(small, a, large)


def _da_bias_tiles(t5_table):
    period = DA_BQ + DA_BK
    m = np.arange(period)
    rel = np.where(m <= DA_BQ, -m, period - m)
    off = DA_BQ * (np.arange(DA_NTILES) - 3)
    u = t5_table[_t5_bucket(jnp.asarray(off[:, None] + rel[None, :], jnp.int32))]
    u = (u * LOG2E).astype(F32).transpose(2, 0, 1)
    flat = jnp.tile(u, (1, 1, DA_BK))[..., :DA_BK * (period - 1)]
    return flat.reshape(DA_HEADS, DA_NTILES, DA_BK, period - 1)[..., :DA_BQ]


def _silu(g):
    return g / (1.0 + jnp.exp(-g))


def _outproj_kernel(oa_ref, ob_ref, ga_ref, gb_ref, x_ref, w_ref, fg_ref, o_ref, *, final):
    ya = (oa_ref[...].astype(F32) * _silu(ga_ref[...].astype(F32))).astype(BF16)
    yb = (ob_ref[...].astype(F32) * _silu(gb_ref[...].astype(F32))).astype(BF16)
    acc = jnp.dot(ya, w_ref[:NA_WIDTH, :], preferred_element_type=F32)
    acc = acc + jnp.dot(yb, w_ref[NA_WIDTH:, :], preferred_element_type=F32)
    xn = x_ref[...] + acc
    if final:
        ms = jnp.mean(xn * xn, axis=-1, keepdims=True)
        xn = xn * lax.rsqrt(ms + NORM_EPS) * fg_ref[...]
    o_ref[...] = xn


def _outproj(out_a, out_b, proj2, x2, w, final_g, final):
    m, d = x2.shape
    ga_blk = 3 * NA_WIDTH // NA_WIDTH
    gb_blk = (4 * NA_WIDTH + 3 * DA_WIDTH) // DA_WIDTH
    return pl.pallas_call(
        functools.partial(_outproj_kernel, final=final),
        grid=(m // PROJ_BM,),
        in_specs=[
            pl.BlockSpec((PROJ_BM, NA_WIDTH), lambda i: (i, 0)),
            pl.BlockSpec((PROJ_BM, DA_WIDTH), lambda i: (i, 0)),
            pl.BlockSpec((PROJ_BM, NA_WIDTH), lambda i: (i, ga_blk)),
            pl.BlockSpec((PROJ_BM, DA_WIDTH), lambda i: (i, gb_blk)),
            pl.BlockSpec((PROJ_BM, d), lambda i: (i, 0)),
            pl.BlockSpec((MIX_WIDTH, d), lambda i: (0, 0)),
            pl.BlockSpec((1, d), lambda i: (0, 0)),
        ],
        out_specs=pl.BlockSpec((PROJ_BM, d), lambda i: (i, 0)),
        out_shape=jax.ShapeDtypeStruct((m, d), F32),
        input_output_aliases={4: 0},
        compiler_params=pltpu.CompilerParams(
            dimension_semantics=("arbitrary",), vmem_limit_bytes=VMEM_LIMIT),
        name="outproj",
    )(out_a, out_b, proj2, proj2, x2, w, final_g)


def kernel(x, norm_g, w_in, na_rpb, lambda_q1, lambda_k1, lambda_q2, lambda_k2, subln_g, t5_table, w_out, final_g):
    b, s, d = x.shape
    depth = w_in.shape[0]
    assert s % DA_BK == 0 and s % GRID_W == 0 and (b * s) % PROJ_BM == 0
    rows = s // GRID_W

    w_in_b = _fold_logit_scale(w_in.astype(F32))
    w_out_b = w_out.astype(BF16)
    na_bias = _na_bias_tables(na_rpb.astype(F32), rows)
    da_bias = _da_bias_tiles(t5_table.astype(F32))
    lamv = jnp.stack([lambda_q1, lambda_k1, lambda_q2, lambda_k2], axis=1).astype(F32)
    fg = final_g.reshape(1, d).astype(F32)

    x2 = x.reshape(b * s, d)
    for l in range(depth):
        lam_init = 0.8 - 0.6 * math.exp(-0.3 * l)
        proj2 = _inproj(x2, norm_g[l].reshape(1, d).astype(F32), w_in_b[l])
        proj = proj2.reshape(b, s, IN_COLS)
        out_a = _na_attention(proj, na_bias[l])
        out_b = _da_attention(proj, lamv[l], da_bias, subln_g[l].reshape(DA_V_DIM, 1).astype(F32), lam_init)
        x2 = _outproj(out_a.reshape(b * s, NA_WIDTH), out_b.reshape(b * s, DA_WIDTH), proj2, x2,
                      w_out_b[l], fg, final=(l == depth - 1))
    return x2.reshape(b, s, d)
```

```python
import functools
import math

import jax
import jax.numpy as jnp
import numpy as np
from jax import lax
from jax.experimental import pallas as pl
from jax.experimental.pallas import tpu as pltpu

F32 = jnp.float32
BF16 = jnp.bfloat16

GRID_W = 64
NA_HEADS = 8
NA_HEAD_DIM = 64
NA_WIDTH = NA_HEADS * NA_HEAD_DIM
NA_WIN_R = 8
NA_WIN_C = 16
DA_HEADS = 4
DA_HEAD_DIM = 64
DA_V_DIM = 2 * DA_HEAD_DIM
DA_WIDTH = DA_HEADS * DA_V_DIM
MIX_WIDTH = NA_WIDTH + DA_WIDTH
IN_COLS = 4 * NA_WIDTH + 4 * DA_WIDTH
T5_BUCKETS = 32
T5_MAX_EXACT = 8
T5_MAX_DIST = 128
NORM_EPS = 1e-6
SUBLN_EPS = 1e-5

LANES = 128
SUBLANES = 8
NEG_BIG = -1e30
LOG2E = math.log2(math.e)

PROJ_BM = 512
PROJ_BN = 1024
NA_RB = 4
NA_KG = 3
NA_UNROLL = 7
DA_BQ = 256
DA_BK = 512
DA_ONES_ROWS = 16
DA_NTILES = 6
VMEM_LIMIT = 48 * 1024 * 1024

_QA, _KA, _VA = 0, NA_WIDTH // LANES, 2 * NA_WIDTH // LANES
_QB = 4 * NA_WIDTH // LANES
_KB = _QB + DA_WIDTH // LANES
_VB = _KB + DA_WIDTH // LANES


def _inproj_kernel(x_ref, g_ref, w_ref, o_ref):
    x = x_ref[...]
    ms = jnp.mean(x * x, axis=-1, keepdims=True)
    h = (x * lax.rsqrt(ms + NORM_EPS) * g_ref[...]).astype(BF16)
    for n in range(IN_COLS // PROJ_BN):
        cols = slice(n * PROJ_BN, (n + 1) * PROJ_BN)
        o_ref[:, cols] = jnp.dot(h, w_ref[:, cols], preferred_element_type=F32).astype(BF16)


def _inproj(x2, g, w):
    m, d = x2.shape
    return pl.pallas_call(
        _inproj_kernel,
        grid=(m // PROJ_BM,),
        in_specs=[
            pl.BlockSpec((PROJ_BM, d), lambda i: (i, 0)),
            pl.BlockSpec((1, d), lambda i: (0, 0)),
            pl.BlockSpec((d, IN_COLS), lambda i: (0, 0)),
        ],
        out_specs=pl.BlockSpec((PROJ_BM, IN_COLS), lambda i: (i, 0)),
        out_shape=jax.ShapeDtypeStruct((m, IN_COLS), BF16),
        compiler_params=pltpu.CompilerParams(
            dimension_semantics=("arbitrary",), vmem_limit_bytes=VMEM_LIMIT),
        name="inproj",
    )(x2, g, w)


def _fold_logit_scale(w_in):
    col = np.ones((IN_COLS,), np.float32)
    col[0:NA_WIDTH] = NA_HEAD_DIM ** -0.5 * LOG2E
    col[4 * NA_WIDTH:4 * NA_WIDTH + DA_WIDTH] = DA_HEAD_DIM ** -0.5 * LOG2E
    return (w_in * col).astype(BF16)


def _na_block_table(kind):
    delta, rs_rel = {
        "first": (0, lambda x: 0),
        "mid": (-(NA_WIN_R // 2), lambda x: x - NA_WIN_R // 2),
        "last": (-NA_WIN_R, lambda x: NA_RB - NA_WIN_R),
    }[kind]
    table = {}
    for i in range(NA_KG * NA_RB):
        for x in range(NA_RB):
            if rs_rel(x) <= delta + i < rs_rel(x) + NA_WIN_R:
                table[(i, x)] = delta + i - x + (NA_WIN_R - 1)
    return table


def _na_kernel(q_ref, k_ref, v_ref, t_ref, o_ref, vt_scr, *, rows):
    ngroups = rows // NA_RB
    gk = NA_RB * GRID_W
    lo = lax.broadcasted_iota(jnp.int32, (1, LANES), 1) < NA_HEAD_DIM

    for g in range(ngroups):
        vt_scr[g] = v_ref[0, g * gk:(g + 1) * gk, :].T

    def block(m, g0, kind):
        table = _na_block_table(kind)
        live = [g for g in range(NA_KG) if any((NA_RB * g + ii, x) in table
                                               for ii in range(NA_RB) for x in range(NA_RB))]
        qstart = pl.multiple_of(m * gk, gk)
        q = q_ref[0, pl.ds(qstart, gk), :]
        zero = jnp.zeros((GRID_W, LANES), q.dtype)
        pieces = []
        for x in range(NA_RB):
            qx = q[x * GRID_W:(x + 1) * GRID_W]
            pieces += [jnp.where(lo, qx, zero), jnp.where(lo, zero, qx)]
        qq_t = jnp.concatenate(pieces, axis=0).T
        s = {g: jnp.dot(k_ref[0, pl.ds(pl.multiple_of((g0 + g) * gk, gk), gk), :], qq_t,
                        preferred_element_type=F32) for g in live}
        p_blocks, l_parts = {}, []
        for x in range(NA_RB):
            blks = {}
            for i in range(NA_KG * NA_RB):
                if (i, x) in table:
                    g, ii = divmod(i, NA_RB)
                    sb = s[g][ii * GRID_W:(ii + 1) * GRID_W, x * LANES:(x + 1) * LANES]
                    blks[i] = sb + t_ref[0, table[(i, x)]]
            mx = functools.reduce(jnp.maximum, blks.values())
            mx = jnp.max(mx, axis=0, keepdims=True)
            lx = None
            for i, sb in blks.items():
                p = jnp.exp2(sb - mx)
                p_blocks[(i, x)] = p
                lx = p if lx is None else lx + p
            l_parts.append(jnp.sum(lx, axis=0, keepdims=True))
        inv_l = 1.0 / jnp.concatenate(l_parts, axis=1)
        zero_blk = jnp.zeros((GRID_W, LANES), F32)
        o_t = None
        for g in live:
            p_rows = [jnp.concatenate([p_blocks.get((NA_RB * g + ii, x), zero_blk) for x in range(NA_RB)], axis=1)
                      for ii in range(NA_RB)]
            pg = jnp.concatenate(p_rows, axis=0).astype(BF16)
            d = jnp.dot(vt_scr[g0 + g], pg, preferred_element_type=F32)
            o_t = d if o_t is None else o_t + d
        o = (o_t * inv_l).T
        for x in range(NA_RB):
            top = o[x * LANES:x * LANES + GRID_W]
            bot = o[x * LANES + GRID_W:(x + 1) * LANES]
            o_ref[0, pl.ds(qstart + x * GRID_W, GRID_W), :] = jnp.where(lo, top, bot).astype(BF16)

    block(0, 0, "first")

    def mid(m, carry):
        block(m, m - 1, "mid")
        return carry

    lax.fori_loop(1, ngroups - 1, mid, 0, unroll=math.gcd(NA_UNROLL, ngroups - 2))
    block(ngroups - 1, ngroups - NA_KG, "last")


def _na_attention(proj, na_table):
    b, s, _ = proj.shape
    rows = s // GRID_W
    assert rows % NA_RB == 0 and rows >= NA_KG * NA_RB
    npairs = NA_HEADS // 2
    blk = lambda off: pl.BlockSpec((1, s, LANES), lambda hp, bi: (bi, 0, off + hp))
    return pl.pallas_call(
        functools.partial(_na_kernel, rows=rows),
        grid=(npairs, b),
        in_specs=[
            blk(_QA), blk(_KA), blk(_VA),
            pl.BlockSpec((1, 2 * NA_WIN_R - 1, GRID_W, LANES), lambda hp, bi: (hp, 0, 0, 0)),
        ],
        out_specs=pl.BlockSpec((1, s, LANES), lambda hp, bi: (bi, 0, hp)),
        out_shape=jax.ShapeDtypeStruct((b, s, NA_WIDTH), BF16),
        scratch_shapes=[pltpu.VMEM((rows // NA_RB, LANES, NA_RB * GRID_W), BF16)],
        compiler_params=pltpu.CompilerParams(
            dimension_semantics=("arbitrary", "arbitrary"), vmem_limit_bytes=VMEM_LIMIT),
        name="na_attn",
    )(proj, proj, proj, na_table)


def _na_bias_tables(na_rpb):
    nl = na_rpb.shape[0]
    c = np.arange(GRID_W)[:, None]
    j = np.arange(GRID_W)[None, :]
    cs = np.clip(c - NA_WIN_C // 2, 0, GRID_W - NA_WIN_C)
    inside = (j >= cs) & (j < cs + NA_WIN_C)
    col_sel = ((j - c + (NA_WIN_C - 1))[:, :, None] == np.arange(2 * NA_WIN_C - 1)) & inside[:, :, None]
    rpb = (na_rpb * LOG2E).reshape(nl, NA_HEADS // 2, 2, 2 * NA_WIN_R - 1, 2 * NA_WIN_C - 1)
    t = jnp.einsum("lpeab,cjb->lpajec", rpb, col_sel.astype(np.float32), precision=lax.Precision.HIGHEST)
    t = t + np.where(inside, 0.0, NEG_BIG).astype(np.float32).T[:, None, :]
    return t.reshape(nl, NA_HEADS // 2, 2 * NA_WIN_R - 1, GRID_W, LANES)


def _da_kernel(lamv_ref, q_ref, k_ref, v_ref, bias_ref, g_ref, o_ref, vt_scr, s_even, s_odd, *, seq, lam_init):
    nchunks = seq // DA_BK
    nq = seq // DA_BQ
    lo = lax.broadcasted_iota(jnp.int32, (1, LANES), 1) < DA_HEAD_DIM
    nsub = DA_BK // SUBLANES

    lv = lamv_ref[...]
    lam = (jnp.exp(jnp.sum(lv[0:1] * lv[1:2], axis=-1, keepdims=True))
           - jnp.exp(jnp.sum(lv[2:3] * lv[3:4], axis=-1, keepdims=True)) + lam_init)

    for c in range(nchunks):
        vt_scr[c, :DA_V_DIM, :] = v_ref[0, c * DA_BK:(c + 1) * DA_BK, :].T
        vt_scr[c, DA_V_DIM:, :] = jnp.ones((DA_ONES_ROWS, DA_BK), BF16)

    def logits(qi, s_scr):
        q = q_ref[0, pl.ds(pl.multiple_of(qi * DA_BQ, DA_BQ), DA_BQ), :]
        zero = jnp.zeros_like(q)
        q_t = jnp.concatenate([jnp.where(lo, q, zero), jnp.where(lo, zero, q)], axis=0).T
        m8 = None
        for c in range(nchunks):
            s = jnp.dot(k_ref[0, c * DA_BK:(c + 1) * DA_BK, :], q_t, preferred_element_type=F32)
            t = jnp.clip((DA_BK // DA_BQ) * c - qi + 3, 0, DA_NTILES - 1)
            bt = bias_ref[0, t]
            s = s + jnp.concatenate([bt, bt], axis=1)
            s_scr[c] = s
            cm = jnp.max(s.reshape(nsub, SUBLANES, 2 * DA_BQ), axis=0)
            m8 = cm if m8 is None else jnp.maximum(m8, cm)
        return jnp.max(m8, axis=0, keepdims=True)

    def attend(qi, s_scr, m):
        a1 = jnp.zeros((DA_V_DIM + DA_ONES_ROWS, DA_BQ), F32)
        a2 = jnp.zeros((DA_V_DIM + DA_ONES_ROWS, DA_BQ), F32)
        for c in range(nchunks):
            pb = jnp.exp2(s_scr[c] - m).astype(BF16)
            vt = vt_scr[c]
            a1 = a1 + jnp.dot(vt, pb[:, :DA_BQ], preferred_element_type=F32)
            a2 = a2 + jnp.dot(vt, pb[:, DA_BQ:], preferred_element_type=F32)
        l1 = a1[DA_V_DIM:DA_V_DIM + 1]
        l2 = a2[DA_V_DIM:DA_V_DIM + 1]
        o = a1[:DA_V_DIM] / l1 - lam * (a2[:DA_V_DIM] / l2)
        ms = jnp.mean(o * o, axis=0, keepdims=True)
        y = o * lax.rsqrt(ms + SUBLN_EPS) * g_ref[...]
        y = y * (1.0 - lam_init)
        o_ref[0, pl.ds(pl.multiple_of(qi * DA_BQ, DA_BQ), DA_BQ), :] = y.T.astype(BF16)

    m_even = logits(0, s_even)

    def pair(jj, m_even):
        qi = 2 * jj
        m_odd = logits(qi + 1, s_odd)
        attend(qi, s_even, m_even)
        m_next = logits(qi + 2, s_even)
        attend(qi + 1, s_odd, m_odd)
        return m_next

    m_even = lax.fori_loop(0, nq // 2 - 1, pair, m_even)
    m_odd = logits(nq - 1, s_odd)
    attend(nq - 2, s_even, m_even)
    attend(nq - 1, s_odd, m_odd)


def _da_attention(proj, lamv, bias_tiles, subln_col, lam_init):
    b, s, _ = proj.shape
    assert (s // DA_BQ) % 2 == 0
    blk = lambda off: pl.BlockSpec((1, s, LANES), lambda h, bi: (bi, 0, off + h))
    logits_scratch = pltpu.VMEM((s // DA_BK, DA_BK, 2 * DA_BQ), F32)
    return pl.pallas_call(
        functools.partial(_da_kernel, seq=s, lam_init=lam_init),
        grid=(DA_HEADS, b),
        in_specs=[
            pl.BlockSpec((4, DA_HEAD_DIM), lambda h, bi: (0, 0)),
            blk(_QB), blk(_KB), blk(_VB),
            pl.BlockSpec((1, DA_NTILES, DA_BK, DA_BQ), lambda h, bi: (h, 0, 0, 0)),
            pl.BlockSpec((DA_V_DIM, 1), lambda h, bi: (0, 0)),
        ],
        out_specs=pl.BlockSpec((1, s, LANES), lambda h, bi: (bi, 0, h)),
        out_shape=jax.ShapeDtypeStruct((b, s, DA_WIDTH), BF16),
        scratch_shapes=[
            pltpu.VMEM((s // DA_BK, DA_V_DIM + DA_ONES_ROWS, DA_BK), BF16),
            logits_scratch,
            logits_scratch,
        ],
        compiler_params=pltpu.CompilerParams(
            dimension_semantics=("arbitrary", "arbitrary"), vmem_limit_bytes=VMEM_LIMIT),
        name="da_attn",
    )(lamv, proj, proj, proj, bias_tiles, subln_col)


def _t5_bucket(rel):
    n = T5_BUCKETS // 2
    ret = jnp.where(rel > 0, n, 0)
    a = jnp.abs(rel)
    small = a < T5_MAX_EXACT
    af = jnp.maximum(a, 1).astype(jnp.float32)
    large = T5_MAX_EXACT + (jnp.log(af / T5_MAX_EXACT) / math.log(T5_MAX_DIST / T5_MAX_EXACT)
                            * (n - T5_MAX_EXACT)).astype(jnp.int32)
    large = jnp.minimum(large, n - 1)
    return ret + jnp.where(small, a, large)


def _da_bias_tiles(t5_table):
    period = DA_BQ + DA_BK
    m = np.arange(period)
    rel = np.where(m <= DA_BQ, -m, period - m)
    off = DA_BQ * (np.arange(DA_NTILES) - 3)
    u = t5_table[_t5_bucket(jnp.asarray(off[:, None] + rel[None, :], jnp.int32))]
    u = (u * LOG2E).astype(F32).transpose(2, 0, 1)
    flat = jnp.tile(u, (1, 1, DA_BK))[..., :DA_BK * (period - 1)]
    return flat.reshape(DA_HEADS, DA_NTILES, DA_BK, period - 1)[..., :DA_BQ]


def _silu(g):
    return g / (1.0 + jnp.exp(-g))


def _outproj_kernel(oa_ref, ob_ref, ga_ref, gb_ref, x_ref, w_ref, fg_ref, o_ref, *, final):
    ya = (oa_ref[...].astype(F32) * _silu(ga_ref[...].astype(F32))).astype(BF16)
    yb = (ob_ref[...].astype(F32) * _silu(gb_ref[...].astype(F32))).astype(BF16)
    acc = jnp.dot(ya, w_ref[:NA_WIDTH, :], preferred_element_type=F32)
    acc = acc + jnp.dot(yb, w_ref[NA_WIDTH:, :], preferred_element_type=F32)
    xn = x_ref[...] + acc
    if final:
        ms = jnp.mean(xn * xn, axis=-1, keepdims=True)
        xn = xn * lax.rsqrt(ms + NORM_EPS) * fg_ref[...]
    o_ref[...] = xn


def _outproj(out_a, out_b, proj2, x2, w, final_g, final, in_place):
    m, d = x2.shape
    ga_blk = 3 * NA_WIDTH // NA_WIDTH
    gb_blk = (4 * NA_WIDTH + 3 * DA_WIDTH) // DA_WIDTH
    return pl.pallas_call(
        functools.partial(_outproj_kernel, final=final),
        grid=(m // PROJ_BM,),
        in_specs=[
            pl.BlockSpec((PROJ_BM, NA_WIDTH), lambda i: (i, 0)),
            pl.BlockSpec((PROJ_BM, DA_WIDTH), lambda i: (i, 0)),
            pl.BlockSpec((PROJ_BM, NA_WIDTH), lambda i: (i, ga_blk)),
            pl.BlockSpec((PROJ_BM, DA_WIDTH), lambda i: (i, gb_blk)),
            pl.BlockSpec((PROJ_BM, d), lambda i: (i, 0)),
            pl.BlockSpec((MIX_WIDTH, d), lambda i: (0, 0)),
            pl.BlockSpec((1, d), lambda i: (0, 0)),
        ],
        out_specs=pl.BlockSpec((PROJ_BM, d), lambda i: (i, 0)),
        out_shape=jax.ShapeDtypeStruct((m, d), F32),
        input_output_aliases={4: 0} if in_place else {},
        compiler_params=pltpu.CompilerParams(
            dimension_semantics=("arbitrary",), vmem_limit_bytes=VMEM_LIMIT),
        name="outproj",
    )(out_a, out_b, proj2, proj2, x2, w, final_g)


def kernel(x, norm_g, w_in, na_rpb, lambda_q1, lambda_k1, lambda_q2, lambda_k2, subln_g, t5_table, w_out, final_g):
    b, s, d = x.shape
    depth = w_in.shape[0]
    assert s % DA_BK == 0 and s % GRID_W == 0 and (b * s) % PROJ_BM == 0
    rows = s // GRID_W

    w_in_b = _fold_logit_scale(w_in.astype(F32))
    w_out_b = w_out.astype(BF16)
    na_bias = _na_bias_tables(na_rpb.astype(F32))
    da_bias = _da_bias_tiles(t5_table.astype(F32))
    lamv = jnp.stack([lambda_q1, lambda_k1, lambda_q2, lambda_k2], axis=1).astype(F32)
    fg = final_g.reshape(1, d).astype(F32)

    x2 = x.reshape(b * s, d)
    for l in range(depth):
        lam_init = 0.8 - 0.6 * math.exp(-0.3 * l)
        proj2 = _inproj(x2, norm_g[l].reshape(1, d).astype(F32), w_in_b[l])
        proj = proj2.reshape(b, s, IN_COLS)
        out_a = _na_attention(proj, na_bias[l])
        out_b = _da_attention(proj, lamv[l], da_bias, subln_g[l].reshape(DA_V_DIM, 1).astype(F32), lam_init)
        x2 = _outproj(out_a.reshape(b * s, NA_WIDTH), out_b.reshape(b * s, DA_WIDTH), proj2, x2,
                      w_out_b[l], fg, final=(l == depth - 1), in_place=(l > 0))
    return x2.reshape(b, s, d)
```

```python
import functools
import math

import jax
import jax.numpy as jnp
import numpy as np
from jax import lax
from jax.experimental import pallas as pl
from jax.experimental.pallas import tpu as pltpu

F32 = jnp.float32
BF16 = jnp.bfloat16

GRID_W = 64
NA_HEADS = 8
NA_HEAD_DIM = 64
NA_WIDTH = NA_HEADS * NA_HEAD_DIM
NA_WIN_R = 8
NA_WIN_C = 16
DA_HEADS = 4
DA_HEAD_DIM = 64
DA_V_DIM = 2 * DA_HEAD_DIM
DA_WIDTH = DA_HEADS * DA_V_DIM
MIX_WIDTH = NA_WIDTH + DA_WIDTH
IN_COLS = 4 * NA_WIDTH + 4 * DA_WIDTH
T5_BUCKETS = 32
T5_MAX_EXACT = 8
T5_MAX_DIST = 128
NORM_EPS = 1e-6
SUBLN_EPS = 1e-5

LANES = 128
SUBLANES = 8
NEG_BIG = -1e30
LOG2E = math.log2(math.e)

PROJ_BM = 512
PROJ_BN = 1024
NA_RB = 4
NA_KG = 3
NA_UNROLL = 7
DA_BQ = 256
DA_BK = 512
DA_ONES_ROWS = 16
DA_NTILES = 6
DA_NEAR = 3
VMEM_LIMIT = 48 * 1024 * 1024

_QA, _KA, _VA = 0, NA_WIDTH // LANES, 2 * NA_WIDTH // LANES
_QB = 4 * NA_WIDTH // LANES
_KB = _QB + DA_WIDTH // LANES
_VB = _KB + DA_WIDTH // LANES


def _norm_project(x, g_ref, w_ref, o_ref):
    ms = jnp.mean(x * x, axis=-1, keepdims=True)
    h = (x * lax.rsqrt(ms + NORM_EPS) * g_ref[...]).astype(BF16)
    for n in range(IN_COLS // PROJ_BN):
        cols = slice(n * PROJ_BN, (n + 1) * PROJ_BN)
        o_ref[:, cols] = jnp.dot(h, w_ref[:, cols], preferred_element_type=F32).astype(BF16)


def _inproj_kernel(x_ref, g_ref, w_ref, o_ref):
    _norm_project(x_ref[...], g_ref, w_ref, o_ref)


def _inproj(x2, g, w):
    m, d = x2.shape
    return pl.pallas_call(
        _inproj_kernel,
        grid=(m // PROJ_BM,),
        in_specs=[
            pl.BlockSpec((PROJ_BM, d), lambda i: (i, 0)),
            pl.BlockSpec((1, d), lambda i: (0, 0)),
            pl.BlockSpec((d, IN_COLS), lambda i: (0, 0)),
        ],
        out_specs=pl.BlockSpec((PROJ_BM, IN_COLS), lambda i: (i, 0)),
        out_shape=jax.ShapeDtypeStruct((m, IN_COLS), BF16),
        compiler_params=pltpu.CompilerParams(
            dimension_semantics=("arbitrary",), vmem_limit_bytes=VMEM_LIMIT),
        name="inproj",
    )(x2, g, w)


def _fold_logit_scale(w_in):
    col = np.ones((IN_COLS,), np.float32)
    col[0:NA_WIDTH] = NA_HEAD_DIM ** -0.5 * LOG2E
    col[4 * NA_WIDTH:4 * NA_WIDTH + DA_WIDTH] = DA_HEAD_DIM ** -0.5 * LOG2E
    return (w_in * col).astype(BF16)


def _na_block_table(kind):
    delta, rs_rel = {
        "first": (0, lambda x: 0),
        "mid": (-(NA_WIN_R // 2), lambda x: x - NA_WIN_R // 2),
        "last": (-NA_WIN_R, lambda x: NA_RB - NA_WIN_R),
    }[kind]
    table = {}
    for i in range(NA_KG * NA_RB):
        for x in range(NA_RB):
            if rs_rel(x) <= delta + i < rs_rel(x) + NA_WIN_R:
                table[(i, x)] = delta + i - x + (NA_WIN_R - 1)
    return table


def _na_kernel(q_ref, k_ref, v_ref, t_ref, o_ref, vt_scr, *, rows):
    ngroups = rows // NA_RB
    gk = NA_RB * GRID_W
    lo = lax.broadcasted_iota(jnp.int32, (1, LANES), 1) < NA_HEAD_DIM

    for g in range(ngroups):
        vt_scr[g] = v_ref[0, g * gk:(g + 1) * gk, :].T

    def block(m, g0, kind):
        table = _na_block_table(kind)
        live = [g for g in range(NA_KG) if any((NA_RB * g + ii, x) in table
                                               for ii in range(NA_RB) for x in range(NA_RB))]
        qstart = pl.multiple_of(m * gk, gk)
        q = q_ref[0, pl.ds(qstart, gk), :]
        zero = jnp.zeros((GRID_W, LANES), q.dtype)
        pieces = []
        for x in range(NA_RB):
            qx = q[x * GRID_W:(x + 1) * GRID_W]
            pieces += [jnp.where(lo, qx, zero), jnp.where(lo, zero, qx)]
        qq_t = jnp.concatenate(pieces, axis=0).T
        s = {g: jnp.dot(k_ref[0, pl.ds(pl.multiple_of((g0 + g) * gk, gk), gk), :], qq_t,
                        preferred_element_type=F32) for g in live}
        p_blocks, l_parts = {}, []
        for x in range(NA_RB):
            blks = {}
            for i in range(NA_KG * NA_RB):
                if (i, x) in table:
                    g, ii = divmod(i, NA_RB)
                    sb = s[g][ii * GRID_W:(ii + 1) * GRID_W, x * LANES:(x + 1) * LANES]
                    blks[i] = sb + t_ref[0, table[(i, x)]]
            mx = functools.reduce(jnp.maximum, blks.values())
            mx = jnp.max(mx, axis=0, keepdims=True)
            lx = None
            for i, sb in blks.items():
                p = jnp.exp2(sb - mx)
                p_blocks[(i, x)] = p
                lx = p if lx is None else lx + p
            l_parts.append(jnp.sum(lx, axis=0, keepdims=True))
        inv_l = 1.0 / jnp.concatenate(l_parts, axis=1)
        zero_blk = jnp.zeros((GRID_W, LANES), F32)
        o_t = None
        for g in live:
            p_rows = [jnp.concatenate([p_blocks.get((NA_RB * g + ii, x), zero_blk) for x in range(NA_RB)], axis=1)
                      for ii in range(NA_RB)]
            pg = jnp.concatenate(p_rows, axis=0).astype(BF16)
            d = jnp.dot(vt_scr[g0 + g], pg, preferred_element_type=F32)
            o_t = d if o_t is None else o_t + d
        o = (o_t * inv_l).T
        for x in range(NA_RB):
            top = o[x * LANES:x * LANES + GRID_W]
            bot = o[x * LANES + GRID_W:(x + 1) * LANES]
            o_ref[0, pl.ds(qstart + x * GRID_W, GRID_W), :] = jnp.where(lo, top, bot).astype(BF16)

    block(0, 0, "first")

    def mid(m, carry):
        block(m, m - 1, "mid")
        return carry

    lax.fori_loop(1, ngroups - 1, mid, 0, unroll=math.gcd(NA_UNROLL, ngroups - 2))
    block(ngroups - 1, ngroups - NA_KG, "last")


def _na_attention(proj, na_table):
    b, s, _ = proj.shape
    rows = s // GRID_W
    assert rows % NA_RB == 0 and rows >= NA_KG * NA_RB
    npairs = NA_HEADS // 2
    blk = lambda off: pl.BlockSpec((1, s, LANES), lambda hp, bi: (bi, 0, off + hp))
    return pl.pallas_call(
        functools.partial(_na_kernel, rows=rows),
        grid=(npairs, b),
        in_specs=[
            blk(_QA), blk(_KA), blk(_VA),
            pl.BlockSpec((1, 2 * NA_WIN_R - 1, GRID_W, LANES), lambda hp, bi: (hp, 0, 0, 0)),
        ],
        out_specs=pl.BlockSpec((1, s, LANES), lambda hp, bi: (bi, 0, hp)),
        out_shape=jax.ShapeDtypeStruct((b, s, NA_WIDTH), BF16),
        scratch_shapes=[pltpu.VMEM((rows // NA_RB, LANES, NA_RB * GRID_W), BF16)],
        compiler_params=pltpu.CompilerParams(
            dimension_semantics=("arbitrary", "arbitrary"), vmem_limit_bytes=VMEM_LIMIT),
        name="na_attn",
    )(proj, proj, proj, na_table)


def _na_bias_tables(na_rpb):
    nl = na_rpb.shape[0]
    c = np.arange(GRID_W)[:, None]
    j = np.arange(GRID_W)[None, :]
    cs = np.clip(c - NA_WIN_C // 2, 0, GRID_W - NA_WIN_C)
    inside = (j >= cs) & (j < cs + NA_WIN_C)
    col_sel = ((j - c + (NA_WIN_C - 1))[:, :, None] == np.arange(2 * NA_WIN_C - 1)) & inside[:, :, None]
    rpb = (na_rpb * LOG2E).reshape(nl, NA_HEADS // 2, 2, 2 * NA_WIN_R - 1, 2 * NA_WIN_C - 1)
    t = jnp.einsum("lpeab,cjb->lpajec", rpb, col_sel.astype(np.float32), precision=lax.Precision.HIGHEST)
    t = t + np.where(inside, 0.0, NEG_BIG).astype(np.float32).T[:, None, :]
    return t.reshape(nl, NA_HEADS // 2, 2 * NA_WIN_R - 1, GRID_W, LANES)


def _da_kernel(lamv_ref, q_ref, k_ref, v_ref, bias_ref, g_ref, o_ref, vt_scr, s_even, s_odd, *, seq, lam_init):
    nchunks = seq // DA_BK
    nq = seq // DA_BQ
    lo = lax.broadcasted_iota(jnp.int32, (1, LANES), 1) < DA_HEAD_DIM
    nsub = DA_BK // SUBLANES

    lv = lamv_ref[...]
    lam = (jnp.exp(jnp.sum(lv[0:1] * lv[1:2], axis=-1, keepdims=True))
           - jnp.exp(jnp.sum(lv[2:3] * lv[3:4], axis=-1, keepdims=True)) + lam_init)

    for c in range(nchunks):
        vt_scr[c, :DA_V_DIM, :] = v_ref[0, c * DA_BK:(c + 1) * DA_BK, :].T
        vt_scr[c, DA_V_DIM:, :] = jnp.ones((DA_ONES_ROWS, DA_BK), BF16)

    delta = bias_ref[0, DA_NTILES - 1, 0:1, 0:1]
    ratio = DA_BK // DA_BQ

    def step(qi_next, s_next, qi, s_cur, m_cur):
        if qi_next is not None:
            q = q_ref[0, pl.ds(pl.multiple_of(qi_next * DA_BQ, DA_BQ), DA_BQ), :]
            zero = jnp.zeros_like(q)
            q_t = jnp.concatenate([jnp.where(lo, q, zero), jnp.where(lo, zero, q)], axis=0).T
            own_next = qi_next // ratio
        if qi is not None:
            own = qi // ratio
        m8 = None
        a1 = jnp.zeros((DA_V_DIM + DA_ONES_ROWS, DA_BQ), F32)
        a2 = jnp.zeros((DA_V_DIM + DA_ONES_ROWS, DA_BQ), F32)
        for c in range(nchunks):
            if qi_next is not None:
                cl = lax.rem(own_next + (nchunks - 1) + c, nchunks)
                kc = k_ref[0, pl.ds(pl.multiple_of(cl * DA_BK, DA_BK), DA_BK), :]
                s = jnp.dot(kc, q_t, preferred_element_type=F32)
                if c < DA_NEAR:
                    bt = bias_ref[0, jnp.clip(ratio * cl - qi_next + 3, 0, DA_NTILES - 1)]
                    s = s + jnp.concatenate([bt, bt], axis=1)
                s_next[cl] = s
                cm = jnp.max(s.reshape(nsub, SUBLANES, 2 * DA_BQ), axis=0)
                if c >= DA_NEAR:
                    cm = cm + jnp.where(own_next + c <= nchunks, delta, 0.0)
                m8 = cm if m8 is None else jnp.maximum(m8, cm)
            if qi is not None:
                far = lax.rem(c - own + 1 + nchunks, nchunks) >= DA_NEAR
                m_c = m_cur - jnp.where(jnp.logical_and(far, c > own), delta, 0.0)
                pb = jnp.exp2(s_cur[c] - m_c).astype(BF16)
                vt = vt_scr[c]
                a1 = a1 + jnp.dot(vt, pb[:, :DA_BQ], preferred_element_type=F32)
                a2 = a2 + jnp.dot(vt, pb[:, DA_BQ:], preferred_element_type=F32)
        if qi is not None:
            l1 = a1[DA_V_DIM:DA_V_DIM + 1]
            l2 = a2[DA_V_DIM:DA_V_DIM + 1]
            o = a1[:DA_V_DIM] / l1 - lam * (a2[:DA_V_DIM] / l2)
            ms = jnp.mean(o * o, axis=0, keepdims=True)
            y = o * lax.rsqrt(ms + SUBLN_EPS) * g_ref[...]
            y = y * (1.0 - lam_init)
            o_ref[0, pl.ds(pl.multiple_of(qi * DA_BQ, DA_BQ), DA_BQ), :] = y.T.astype(BF16)
        return None if m8 is None else jnp.max(m8, axis=0, keepdims=True)

    m_even = step(0, s_even, None, None, None)

    def pair(jj, m_even):
        qi = 2 * jj
        m_odd = step(qi + 1, s_odd, qi, s_even, m_even)
        return step(qi + 2, s_even, qi + 1, s_odd, m_odd)

    m_even = lax.fori_loop(0, nq // 2 - 1, pair, m_even)
    m_odd = step(nq - 1, s_odd, nq - 2, s_even, m_even)
    step(None, None, nq - 1, s_odd, m_odd)


def _da_attention(proj, lamv, bias_tiles, subln_col, lam_init):
    b, s, _ = proj.shape
    assert (s // DA_BQ) % 2 == 0 and s // DA_BK >= DA_NEAR
    blk = lambda off: pl.BlockSpec((1, s, LANES), lambda h, bi: (bi, 0, off + h))
    logits_scratch = pltpu.VMEM((s // DA_BK, DA_BK, 2 * DA_BQ), F32)
    return pl.pallas_call(
        functools.partial(_da_kernel, seq=s, lam_init=lam_init),
        grid=(DA_HEADS, b),
        in_specs=[
            pl.BlockSpec((4, DA_HEAD_DIM), lambda h, bi: (0, 0)),
            blk(_QB), blk(_KB), blk(_VB),
            pl.BlockSpec((1, DA_NTILES, DA_BK, DA_BQ), lambda h, bi: (h, 0, 0, 0), pipeline_mode=pl.Buffered(1)),
            pl.BlockSpec((DA_V_DIM, 1), lambda h, bi: (0, 0)),
        ],
        out_specs=pl.BlockSpec((1, s, LANES), lambda h, bi: (bi, 0, h)),
        out_shape=jax.ShapeDtypeStruct((b, s, DA_WIDTH), BF16),
        scratch_shapes=[
            pltpu.VMEM((s // DA_BK, DA_V_DIM + DA_ONES_ROWS, DA_BK), BF16),
            logits_scratch,
            logits_scratch,
        ],
        compiler_params=pltpu.CompilerParams(
            dimension_semantics=("arbitrary", "arbitrary"), vmem_limit_bytes=VMEM_LIMIT),
        name="da_attn",
    )(lamv, proj, proj, proj, bias_tiles, subln_col)


def _t5_bucket(rel):
    n = T5_BUCKETS // 2
    ret = jnp.where(rel > 0, n, 0)
    a = jnp.abs(rel)
    small = a < T5_MAX_EXACT
    af = jnp.maximum(a, 1).astype(jnp.float32)
    large = T5_MAX_EXACT + (jnp.log(af / T5_MAX_EXACT) / math.log(T5_MAX_DIST / T5_MAX_EXACT)
                            * (n - T5_MAX_EXACT)).astype(jnp.int32)
    large = jnp.minimum(large, n - 1)
    return ret + jnp.where(small, a, large)


def _da_bias_tiles(t5_table):
    period = DA_BQ + DA_BK
    m = np.arange(period)
    rel = np.where(m <= DA_BQ, -m, period - m)
    off = DA_BQ * (np.arange(DA_NTILES) - 3)
    u = t5_table[_t5_bucket(jnp.asarray(off[:, None] + rel[None, :], jnp.int32))]
    u = (u * LOG2E).astype(F32).transpose(2, 0, 1)
    u = u - u[:, :1, :1]
    flat = jnp.tile(u, (1, 1, DA_BK))[..., :DA_BK * (period - 1)]
    return flat.reshape(DA_HEADS, DA_NTILES, DA_BK, period - 1)[..., :DA_BQ]


def _silu(g):
    return g / (1.0 + jnp.exp(-g))


def _mix_residual(oa_ref, ob_ref, ga_ref, gb_ref, x_ref, w_ref):
    ya = (oa_ref[...].astype(F32) * _silu(ga_ref[...].astype(F32))).astype(BF16)
    yb = (ob_ref[...].astype(F32) * _silu(gb_ref[...].astype(F32))).astype(BF16)
    acc = jnp.dot(ya, w_ref[:NA_WIDTH, :], preferred_element_type=F32)
    acc = acc + jnp.dot(yb, w_ref[NA_WIDTH:, :], preferred_element_type=F32)
    return x_ref[...] + acc


def _outproj_final_kernel(oa_ref, ob_ref, ga_ref, gb_ref, x_ref, w_ref, fg_ref, o_ref):
    xn = _mix_residual(oa_ref, ob_ref, ga_ref, gb_ref, x_ref, w_ref)
    ms = jnp.mean(xn * xn, axis=-1, keepdims=True)
    o_ref[...] = xn * lax.rsqrt(ms + NORM_EPS) * fg_ref[...]


def _outproj_inproj_kernel(oa_ref, ob_ref, ga_ref, gb_ref, x_ref, wo_ref, g_ref, wi_ref, xo_ref, p_ref):
    xn = _mix_residual(oa_ref, ob_ref, ga_ref, gb_ref, x_ref, wo_ref)
    xo_ref[...] = xn
    _norm_project(xn, g_ref, wi_ref, p_ref)


def _mix_in_specs(d):
    ga_blk = 3 * NA_WIDTH // NA_WIDTH
    gb_blk = (4 * NA_WIDTH + 3 * DA_WIDTH) // DA_WIDTH
    return [
        pl.BlockSpec((PROJ_BM, NA_WIDTH), lambda i: (i, 0)),
        pl.BlockSpec((PROJ_BM, DA_WIDTH), lambda i: (i, 0)),
        pl.BlockSpec((PROJ_BM, NA_WIDTH), lambda i: (i, ga_blk)),
        pl.BlockSpec((PROJ_BM, DA_WIDTH), lambda i: (i, gb_blk)),
        pl.BlockSpec((PROJ_BM, d), lambda i: (i, 0)),
        pl.BlockSpec((MIX_WIDTH, d), lambda i: (0, 0), pipeline_mode=pl.Buffered(1)),
    ]


def _outproj_final(out_a, out_b, proj2, x2, w, final_g):
    m, d = x2.shape
    return pl.pallas_call(
        _outproj_final_kernel,
        grid=(m // PROJ_BM,),
        in_specs=_mix_in_specs(d) + [pl.BlockSpec((1, d), lambda i: (0, 0))],
        out_specs=pl.BlockSpec((PROJ_BM, d), lambda i: (i, 0)),
        out_shape=jax.ShapeDtypeStruct((m, d), F32),
        input_output_aliases={4: 0},
        compiler_params=pltpu.CompilerParams(
            dimension_semantics=("arbitrary",), vmem_limit_bytes=VMEM_LIMIT),
        name="outproj_final",
    )(out_a, out_b, proj2, proj2, x2, w, final_g)


def _outproj_inproj(out_a, out_b, proj2, x2, w_out, g, w_in, in_place):
    m, d = x2.shape
    return pl.pallas_call(
        _outproj_inproj_kernel,
        grid=(m // PROJ_BM,),
        in_specs=_mix_in_specs(d) + [
            pl.BlockSpec((1, d), lambda i: (0, 0)),
            pl.BlockSpec((d, IN_COLS), lambda i: (0, 0), pipeline_mode=pl.Buffered(1)),
        ],
        out_specs=[
            pl.BlockSpec((PROJ_BM, d), lambda i: (i, 0)),
            pl.BlockSpec((PROJ_BM, IN_COLS), lambda i: (i, 0)),
        ],
        out_shape=[jax.ShapeDtypeStruct((m, d), F32), jax.ShapeDtypeStruct((m, IN_COLS), BF16)],
        input_output_aliases={4: 0} if in_place else {},
        compiler_params=pltpu.CompilerParams(
            dimension_semantics=("arbitrary",), vmem_limit_bytes=VMEM_LIMIT),
        name="outproj_inproj",
    )(out_a, out_b, proj2, proj2, x2, w_out, g, w_in)


def kernel(x, norm_g, w_in, na_rpb, lambda_q1, lambda_k1, lambda_q2, lambda_k2, subln_g, t5_table, w_out, final_g):
    b, s, d = x.shape
    depth = w_in.shape[0]
    assert s % DA_BK == 0 and s % GRID_W == 0 and (b * s) % PROJ_BM == 0

    w_in_b = _fold_logit_scale(w_in.astype(F32))
    w_out_b = w_out.astype(BF16)
    na_bias = _na_bias_tables(na_rpb.astype(F32))
    da_bias = _da_bias_tiles(t5_table.astype(F32))
    lamv = jnp.stack([lambda_q1, lambda_k1, lambda_q2, lambda_k2], axis=1).astype(F32)
    norm_rows = norm_g.reshape(depth, 1, d).astype(F32)

    x2 = x.reshape(b * s, d)
    proj2 = _inproj(x2, norm_rows[0], w_in_b[0])
    for l in range(depth):
        lam_init = 0.8 - 0.6 * math.exp(-0.3 * l)
        proj = proj2.reshape(b, s, IN_COLS)
        out_a = _na_attention(proj, na_bias[l]).reshape(b * s, NA_WIDTH)
        out_b = _da_attention(proj, lamv[l], da_bias, subln_g[l].reshape(DA_V_DIM, 1).astype(F32), lam_init)
        out_b = out_b.reshape(b * s, DA_WIDTH)
        if l + 1 < depth:
            x2, proj2 = _outproj_inproj(out_a, out_b, proj2, x2, w_out_b[l], norm_rows[l + 1], w_in_b[l + 1],
                                        in_place=(l > 0))
        else:
            x2 = _outproj_final(out_a, out_b, proj2, x2, w_out_b[l], final_g.reshape(1, d).astype(F32))
    return x2.reshape(b, s, d)
```

```python
import functools
import math

import jax
import jax.numpy as jnp
import numpy as np
from jax import lax
from jax.experimental import pallas as pl
from jax.experimental.pallas import tpu as pltpu

F32 = jnp.float32
BF16 = jnp.bfloat16

GRID_W = 64
NA_HEADS = 8
NA_HEAD_DIM = 64
NA_WIDTH = NA_HEADS * NA_HEAD_DIM
NA_WIN_R = 8
NA_WIN_C = 16
DA_HEADS = 4
DA_HEAD_DIM = 64
DA_V_DIM = 2 * DA_HEAD_DIM
DA_WIDTH = DA_HEADS * DA_V_DIM
MIX_WIDTH = NA_WIDTH + DA_WIDTH
IN_COLS = 4 * NA_WIDTH + 4 * DA_WIDTH
T5_BUCKETS = 32
T5_MAX_EXACT = 8
T5_MAX_DIST = 128
NORM_EPS = 1e-6
SUBLN_EPS = 1e-5

LANES = 128
SUBLANES = 8
NEG_BIG = -1e30
LOG2E = math.log2(math.e)

PROJ_BM = 512
PROJ_BN = 1024
NA_RB = 4
NA_KG = 3
NA_ONES_ROWS = 16
NA_UNROLL = 7
DA_BQ = 256
DA_BK = 512
DA_ONES_ROWS = 16
DA_NTILES = 6
DA_NEAR = 3
VMEM_LIMIT = 48 * 1024 * 1024

_QA, _KA, _VA = 0, NA_WIDTH // LANES, 2 * NA_WIDTH // LANES
_QB = 4 * NA_WIDTH // LANES
_KB = _QB + DA_WIDTH // LANES
_VB = _KB + DA_WIDTH // LANES


def _norm_project(x, g_ref, w_ref, o_ref):
    ms = jnp.mean(x * x, axis=-1, keepdims=True)
    h = (x * lax.rsqrt(ms + NORM_EPS) * g_ref[...]).astype(BF16)
    for n in range(IN_COLS // PROJ_BN):
        cols = slice(n * PROJ_BN, (n + 1) * PROJ_BN)
        o_ref[:, cols] = jnp.dot(h, w_ref[:, cols], preferred_element_type=F32).astype(BF16)


def _inproj_kernel(x_ref, g_ref, w_ref, o_ref):
    _norm_project(x_ref[...], g_ref, w_ref, o_ref)


def _inproj(x2, g, w):
    m, d = x2.shape
    return pl.pallas_call(
        _inproj_kernel,
        grid=(m // PROJ_BM,),
        in_specs=[
            pl.BlockSpec((PROJ_BM, d), lambda i: (i, 0)),
            pl.BlockSpec((1, d), lambda i: (0, 0)),
            pl.BlockSpec((d, IN_COLS), lambda i: (0, 0)),
        ],
        out_specs=pl.BlockSpec((PROJ_BM, IN_COLS), lambda i: (i, 0)),
        out_shape=jax.ShapeDtypeStruct((m, IN_COLS), BF16),
        compiler_params=pltpu.CompilerParams(
            dimension_semantics=("arbitrary",), vmem_limit_bytes=VMEM_LIMIT),
        name="inproj",
    )(x2, g, w)


def _fold_logit_scale(w_in):
    col = np.ones((IN_COLS,), np.float32)
    col[0:NA_WIDTH] = NA_HEAD_DIM ** -0.5 * LOG2E
    col[4 * NA_WIDTH:4 * NA_WIDTH + DA_WIDTH] = DA_HEAD_DIM ** -0.5 * LOG2E
    return (w_in * col).astype(BF16)


def _na_block_table(kind):
    delta, rs_rel = {
        "first": (0, lambda x: 0),
        "mid": (-(NA_WIN_R // 2), lambda x: x - NA_WIN_R // 2),
        "last": (-NA_WIN_R, lambda x: NA_RB - NA_WIN_R),
    }[kind]
    table = {}
    for i in range(NA_KG * NA_RB):
        for x in range(NA_RB):
            if rs_rel(x) <= delta + i < rs_rel(x) + NA_WIN_R:
                table[(i, x)] = delta + i - x + (NA_WIN_R - 1)
    return table


def _na_kernel(q_ref, k_ref, v_ref, t_ref, o_ref, vt_scr, *, rows):
    ngroups = rows // NA_RB
    gk = NA_RB * GRID_W
    lo = lax.broadcasted_iota(jnp.int32, (1, LANES), 1) < NA_HEAD_DIM

    for g in range(ngroups):
        vt_scr[g, :LANES, :] = v_ref[0, g * gk:(g + 1) * gk, :].T
        vt_scr[g, LANES:, :] = jnp.ones((NA_ONES_ROWS, gk), BF16)

    def block(m, g0, kind):
        table = _na_block_table(kind)
        live = [g for g in range(NA_KG) if any((NA_RB * g + ii, x) in table
                                               for ii in range(NA_RB) for x in range(NA_RB))]
        qstart = pl.multiple_of(m * gk, gk)
        q = q_ref[0, pl.ds(qstart, gk), :]
        zero = jnp.zeros((GRID_W, LANES), q.dtype)
        pieces = []
        for x in range(NA_RB):
            qx = q[x * GRID_W:(x + 1) * GRID_W]
            pieces += [jnp.where(lo, qx, zero), jnp.where(lo, zero, qx)]
        qq_t = jnp.concatenate(pieces, axis=0).T
        kw = k_ref[0, pl.ds(pl.multiple_of((g0 + live[0]) * gk, gk), len(live) * gk), :]
        s_all = jnp.dot(kw, qq_t, preferred_element_type=F32)
        s = {g: s_all[n * gk:(n + 1) * gk] for n, g in enumerate(live)}
        p_blocks = {}
        for x in range(NA_RB):
            blks = {}
            for i in range(NA_KG * NA_RB):
                if (i, x) in table:
                    g, ii = divmod(i, NA_RB)
                    sb = s[g][ii * GRID_W:(ii + 1) * GRID_W, x * LANES:(x + 1) * LANES]
                    blks[i] = sb + t_ref[0, table[(i, x)]]
            mx = functools.reduce(jnp.maximum, blks.values())
            mx = jnp.max(mx, axis=0, keepdims=True)
            for i, sb in blks.items():
                p_blocks[(i, x)] = jnp.exp2(sb - mx)
        zero_blk = jnp.zeros((GRID_W, LANES), F32)
        p_rows = [jnp.concatenate([p_blocks.get((NA_RB * g + ii, x), zero_blk) for x in range(NA_RB)], axis=1)
                  for g in live for ii in range(NA_RB)]
        p_all = jnp.concatenate(p_rows, axis=0).astype(BF16)
        vt = jnp.concatenate([vt_scr[g0 + g] for g in live], axis=1)
        o_t = jnp.dot(vt, p_all, preferred_element_type=F32)
        inv_l = 1.0 / o_t[LANES:LANES + 1]
        o = (o_t[:LANES] * inv_l).T
        for x in range(NA_RB):
            top = o[x * LANES:x * LANES + GRID_W]
            bot = o[x * LANES + GRID_W:(x + 1) * LANES]
            o_ref[0, pl.ds(qstart + x * GRID_W, GRID_W), :] = jnp.where(lo, top, bot).astype(BF16)

    block(0, 0, "first")

    def mid(m, carry):
        block(m, m - 1, "mid")
        return carry

    lax.fori_loop(1, ngroups - 1, mid, 0, unroll=math.gcd(NA_UNROLL, ngroups - 2))
    block(ngroups - 1, ngroups - NA_KG, "last")


def _na_attention(proj, na_table):
    b, s, _ = proj.shape
    rows = s // GRID_W
    assert rows % NA_RB == 0 and rows >= NA_KG * NA_RB
    npairs = NA_HEADS // 2
    blk = lambda off: pl.BlockSpec((1, s, LANES), lambda hp, bi: (bi, 0, off + hp))
    return pl.pallas_call(
        functools.partial(_na_kernel, rows=rows),
        grid=(npairs, b),
        in_specs=[
            blk(_QA), blk(_KA), blk(_VA),
            pl.BlockSpec((1, 2 * NA_WIN_R - 1, GRID_W, LANES), lambda hp, bi: (hp, 0, 0, 0)),
        ],
        out_specs=pl.BlockSpec((1, s, LANES), lambda hp, bi: (bi, 0, hp)),
        out_shape=jax.ShapeDtypeStruct((b, s, NA_WIDTH), BF16),
        scratch_shapes=[pltpu.VMEM((rows // NA_RB, LANES + NA_ONES_ROWS, NA_RB * GRID_W), BF16)],
        compiler_params=pltpu.CompilerParams(
            dimension_semantics=("arbitrary", "arbitrary"), vmem_limit_bytes=VMEM_LIMIT),
        name="na_attn",
    )(proj, proj, proj, na_table)


def _na_bias_tables(na_rpb):
    nl = na_rpb.shape[0]
    c = np.arange(GRID_W)[:, None]
    j = np.arange(GRID_W)[None, :]
    cs = np.clip(c - NA_WIN_C // 2, 0, GRID_W - NA_WIN_C)
    inside = (j >= cs) & (j < cs + NA_WIN_C)
    col_sel = ((j - c + (NA_WIN_C - 1))[:, :, None] == np.arange(2 * NA_WIN_C - 1)) & inside[:, :, None]
    rpb = (na_rpb * LOG2E).reshape(nl, NA_HEADS // 2, 2, 2 * NA_WIN_R - 1, 2 * NA_WIN_C - 1)
    t = jnp.einsum("lpeab,cjb->lpajec", rpb, col_sel.astype(np.float32), precision=lax.Precision.HIGHEST)
    t = t + np.where(inside, 0.0, NEG_BIG).astype(np.float32).T[:, None, :]
    return t.reshape(nl, NA_HEADS // 2, 2 * NA_WIN_R - 1, GRID_W, LANES)


def _da_kernel(lamv_ref, q_ref, k_ref, v_ref, u_ref, g_ref, o_ref, vt_scr, s_even, s_odd, tiles_scr,
               *, seq, lam_init):
    nchunks = seq // DA_BK
    nq = seq // DA_BQ
    lo = lax.broadcasted_iota(jnp.int32, (1, LANES), 1) < DA_HEAD_DIM
    nsub = DA_BK // SUBLANES

    lv = lamv_ref[...]
    lam = (jnp.exp(jnp.sum(lv[0:1] * lv[1:2], axis=-1, keepdims=True))
           - jnp.exp(jnp.sum(lv[2:3] * lv[3:4], axis=-1, keepdims=True)) + lam_init)

    for c in range(nchunks):
        vt_scr[c, :DA_V_DIM, :] = v_ref[0, c * DA_BK:(c + 1) * DA_BK, :].T
        vt_scr[c, DA_V_DIM:, :] = jnp.ones((DA_ONES_ROWS, DA_BK), BF16)

    @pl.when(pl.program_id(1) == 0)
    def _():
        for t in range(DA_NTILES):
            rows = jnp.broadcast_to(u_ref[0, t], (DA_BK, DA_BQ + DA_BK))
            tiles_scr[t] = pltpu.roll(rows, 0, 1, stride=1, stride_axis=0)[:, :DA_BQ]

    delta = u_ref[0, DA_NTILES - 1, 0:1, 0:1]
    ratio = DA_BK // DA_BQ

    def step(qi_next, s_next, qi, s_cur, m_cur, qi_done, acc_done):
        if qi_done is not None:
            a1, a2 = acc_done
            l1 = a1[DA_V_DIM:DA_V_DIM + 1]
            l2 = a2[DA_V_DIM:DA_V_DIM + 1]
            o = a1[:DA_V_DIM] / l1 - lam * (a2[:DA_V_DIM] / l2)
            ms = jnp.mean(o * o, axis=0, keepdims=True)
            y = o * lax.rsqrt(ms + SUBLN_EPS) * g_ref[...]
            y = y * (1.0 - lam_init)
            o_ref[0, pl.ds(pl.multiple_of(qi_done * DA_BQ, DA_BQ), DA_BQ), :] = y.T.astype(BF16)
        if qi_next is not None:
            q = q_ref[0, pl.ds(pl.multiple_of(qi_next * DA_BQ, DA_BQ), DA_BQ), :]
            zero = jnp.zeros_like(q)
            q_t = jnp.concatenate([jnp.where(lo, q, zero), jnp.where(lo, zero, q)], axis=0).T
            own_next = qi_next // ratio
        if qi is not None:
            own = qi // ratio
        m8 = None
        a1 = jnp.zeros((DA_V_DIM + DA_ONES_ROWS, DA_BQ), F32)
        a2 = jnp.zeros((DA_V_DIM + DA_ONES_ROWS, DA_BQ), F32)
        for c in range(nchunks):
            if qi_next is not None:
                cl = lax.rem(own_next + (nchunks - 1) + c, nchunks)
                kc = k_ref[0, pl.ds(pl.multiple_of(cl * DA_BK, DA_BK), DA_BK), :]
                s = jnp.dot(kc, q_t, preferred_element_type=F32)
                if c < DA_NEAR:
                    bt = tiles_scr[jnp.clip(ratio * cl - qi_next + 3, 0, DA_NTILES - 1)]
                    s = s + jnp.concatenate([bt, bt], axis=1)
                s_next[cl] = s
                cm = jnp.max(s.reshape(nsub, SUBLANES, 2 * DA_BQ), axis=0)
                if c >= DA_NEAR:
                    cm = cm + jnp.where(own_next + c <= nchunks, delta, 0.0)
                m8 = cm if m8 is None else jnp.maximum(m8, cm)
            if qi is not None:
                far = lax.rem(c - own + 1 + nchunks, nchunks) >= DA_NEAR
                m_c = m_cur - jnp.where(jnp.logical_and(far, c > own), delta, 0.0)
                pb = jnp.exp2(s_cur[c] - m_c).astype(BF16)
                vt = vt_scr[c]
                a1 = a1 + jnp.dot(vt, pb[:, :DA_BQ], preferred_element_type=F32)
                a2 = a2 + jnp.dot(vt, pb[:, DA_BQ:], preferred_element_type=F32)
        return (None if m8 is None else jnp.max(m8, axis=0, keepdims=True)), (a1, a2)

    m0, _ = step(0, s_even, None, None, None, None, None)
    m1, acc0 = step(1, s_odd, 0, s_even, m0, None, None)

    def pair(jj, carry):
        m_prev, acc_prev = carry
        k = 2 * jj
        m_k, acc_km1 = step(k, s_even, k - 1, s_odd, m_prev, k - 2, acc_prev)
        return step(k + 1, s_odd, k, s_even, m_k, k - 1, acc_km1)

    m_last, acc_prev = lax.fori_loop(1, nq // 2, pair, (m1, acc0))
    _, acc_last = step(None, None, nq - 1, s_odd, m_last, nq - 2, acc_prev)
    step(None, None, None, None, None, nq - 1, acc_last)


def _da_attention(proj, lamv, bias_rows, subln_col, lam_init):
    b, s, _ = proj.shape
    assert (s // DA_BQ) % 2 == 0 and s // DA_BK >= DA_NEAR
    blk = lambda off: pl.BlockSpec((1, s, LANES), lambda h, bi: (bi, 0, off + h))
    logits_scratch = pltpu.VMEM((s // DA_BK, DA_BK, 2 * DA_BQ), F32)
    return pl.pallas_call(
        functools.partial(_da_kernel, seq=s, lam_init=lam_init),
        grid=(DA_HEADS, b),
        in_specs=[
            pl.BlockSpec((4, DA_HEAD_DIM), lambda h, bi: (0, 0)),
            blk(_QB), blk(_KB), blk(_VB),
            pl.BlockSpec((1, DA_NTILES, 1, DA_BQ + DA_BK), lambda h, bi: (h, 0, 0, 0)),
            pl.BlockSpec((DA_V_DIM, 1), lambda h, bi: (0, 0)),
        ],
        out_specs=pl.BlockSpec((1, s, LANES), lambda h, bi: (bi, 0, h)),
        out_shape=jax.ShapeDtypeStruct((b, s, DA_WIDTH), BF16),
        scratch_shapes=[
            pltpu.VMEM((s // DA_BK, DA_V_DIM + DA_ONES_ROWS, DA_BK), BF16),
            logits_scratch,
            logits_scratch,
            pltpu.VMEM((DA_NTILES, DA_BK, DA_BQ), F32),
        ],
        compiler_params=pltpu.CompilerParams(
            dimension_semantics=("arbitrary", "arbitrary"), vmem_limit_bytes=VMEM_LIMIT),
        name="da_attn",
    )(lamv, proj, proj, proj, bias_rows, subln_col)


def _t5_bucket(rel):
    n = T5_BUCKETS // 2
    ret = jnp.where(rel > 0, n, 0)
    a = jnp.abs(rel)
    small = a < T5_MAX_EXACT
    af = jnp.maximum(a, 1).astype(jnp.float32)
    large = T5_MAX_EXACT + (jnp.log(af / T5_MAX_EXACT) / math.log(T5_MAX_DIST / T5_MAX_EXACT)
                            * (n - T5_MAX_EXACT)).astype(jnp.int32)
    large = jnp.minimum(large, n - 1)
    return ret + jnp.where(small, a, large)


def _da_bias_rows(t5_table):
    period = DA_BQ + DA_BK
    m = np.arange(period)
    rel = np.where(m <= DA_BQ, -m, period - m)
    off = DA_BQ * (np.arange(DA_NTILES) - 3)
    u = t5_table[_t5_bucket(jnp.asarray(off[:, None] + rel[None, :], jnp.int32))]
    u = (u * LOG2E).astype(F32).transpose(2, 0, 1)
    u = u - u[:, :1, :1]
    return u.reshape(DA_HEADS, DA_NTILES, 1, period)


def _silu(g):
    return g / (1.0 + jnp.exp(-g))


def _mix_residual(oa_ref, ob_ref, ga_ref, gb_ref, x_ref, w_ref):
    ya = (oa_ref[...].astype(F32) * _silu(ga_ref[...].astype(F32))).astype(BF16)
    yb = (ob_ref[...].astype(F32) * _silu(gb_ref[...].astype(F32))).astype(BF16)
    acc = jnp.dot(ya, w_ref[:NA_WIDTH, :], preferred_element_type=F32)
    acc = acc + jnp.dot(yb, w_ref[NA_WIDTH:, :], preferred_element_type=F32)
    return x_ref[...] + acc


def _outproj_final_kernel(oa_ref, ob_ref, ga_ref, gb_ref, x_ref, w_ref, fg_ref, o_ref):
    xn = _mix_residual(oa_ref, ob_ref, ga_ref, gb_ref, x_ref, w_ref)
    ms = jnp.mean(xn * xn, axis=-1, keepdims=True)
    o_ref[...] = xn * lax.rsqrt(ms + NORM_EPS) * fg_ref[...]


def _outproj_inproj_kernel(oa_ref, ob_ref, ga_ref, gb_ref, x_ref, wo_ref, g_ref, wi_ref, xo_ref, p_ref):
    xn = _mix_residual(oa_ref, ob_ref, ga_ref, gb_ref, x_ref, wo_ref)
    xo_ref[...] = xn
    _norm_project(xn, g_ref, wi_ref, p_ref)


def _mix_in_specs(d):
    ga_blk = 3 * NA_WIDTH // NA_WIDTH
    gb_blk = (4 * NA_WIDTH + 3 * DA_WIDTH) // DA_WIDTH
    return [
        pl.BlockSpec((PROJ_BM, NA_WIDTH), lambda i: (i, 0)),
        pl.BlockSpec((PROJ_BM, DA_WIDTH), lambda i: (i, 0)),
        pl.BlockSpec((PROJ_BM, NA_WIDTH), lambda i: (i, ga_blk)),
        pl.BlockSpec((PROJ_BM, DA_WIDTH), lambda i: (i, gb_blk)),
        pl.BlockSpec((PROJ_BM, d), lambda i: (i, 0)),
        pl.BlockSpec((MIX_WIDTH, d), lambda i: (0, 0), pipeline_mode=pl.Buffered(1)),
    ]


def _outproj_final(out_a, out_b, proj2, x2, w, final_g):
    m, d = x2.shape
    return pl.pallas_call(
        _outproj_final_kernel,
        grid=(m // PROJ_BM,),
        in_specs=_mix_in_specs(d) + [pl.BlockSpec((1, d), lambda i: (0, 0))],
        out_specs=pl.BlockSpec((PROJ_BM, d), lambda i: (i, 0)),
        out_shape=jax.ShapeDtypeStruct((m, d), F32),
        input_output_aliases={4: 0},
        compiler_params=pltpu.CompilerParams(
            dimension_semantics=("arbitrary",), vmem_limit_bytes=VMEM_LIMIT),
        name="outproj_final",
    )(out_a, out_b, proj2, proj2, x2, w, final_g)


def _outproj_inproj(out_a, out_b, proj2, x2, w_out, g, w_in, in_place):
    m, d = x2.shape
    return pl.pallas_call(
        _outproj_inproj_kernel,
        grid=(m // PROJ_BM,),
        in_specs=_mix_in_specs(d) + [
            pl.BlockSpec((1, d), lambda i: (0, 0)),
            pl.BlockSpec((d, IN_COLS), lambda i: (0, 0), pipeline_mode=pl.Buffered(1)),
        ],
        out_specs=[
            pl.BlockSpec((PROJ_BM, d), lambda i: (i, 0)),
            pl.BlockSpec((PROJ_BM, IN_COLS), lambda i: (i, 0)),
        ],
        out_shape=[jax.ShapeDtypeStruct((m, d), F32), jax.ShapeDtypeStruct((m, IN_COLS), BF16)],
        input_output_aliases={4: 0} if in_place else {},
        compiler_params=pltpu.CompilerParams(
            dimension_semantics=("arbitrary",), vmem_limit_bytes=VMEM_LIMIT),
        name="outproj_inproj",
    )(out_a, out_b, proj2, proj2, x2, w_out, g, w_in)


def kernel(x, norm_g, w_in, na_rpb, lambda_q1, lambda_k1, lambda_q2, lambda_k2, subln_g, t5_table, w_out, final_g):
    b, s, d = x.shape
    depth = w_in.shape[0]
    assert s % DA_BK == 0 and s % GRID_W == 0 and (b * s) % PROJ_BM == 0

    w_in_b = _fold_logit_scale(w_in.astype(F32))
    w_out_b = w_out.astype(BF16)
    na_bias = _na_bias_tables(na_rpb.astype(F32))
    da_bias = _da_bias_rows(t5_table.astype(F32))
    lamv = jnp.stack([lambda_q1, lambda_k1, lambda_q2, lambda_k2], axis=1).astype(F32)
    norm_rows = norm_g.reshape(depth, 1, d).astype(F32)

    x2 = x.reshape(b * s, d)
    proj2 = _inproj(x2, norm_rows[0], w_in_b[0])
    for l in range(depth):
        lam_init = 0.8 - 0.6 * math.exp(-0.3 * l)
        proj = proj2.reshape(b, s, IN_COLS)
        out_a = _na_attention(proj, na_bias[l]).reshape(b * s, NA_WIDTH)
        out_b = _da_attention(proj, lamv[l], da_bias, subln_g[l].reshape(DA_V_DIM, 1).astype(F32), lam_init)
        out_b = out_b.reshape(b * s, DA_WIDTH)
        if l + 1 < depth:
            x2, proj2 = _outproj_inproj(out_a, out_b, proj2, x2, w_out_b[l], norm_rows[l + 1], w_in_b[l + 1],
                                        in_place=(l > 0))
        else:
            x2 = _outproj_final(out_a, out_b, proj2, x2, w_out_b[l], final_g.reshape(1, d).astype(F32))
    return x2.reshape(b, s, d)
```

```python
import functools
import math

import jax
import jax.numpy as jnp
import numpy as np
from jax import lax
from jax.experimental import pallas as pl
from jax.experimental.pallas import tpu as pltpu

F32 = jnp.float32
BF16 = jnp.bfloat16

GRID_W = 64
NA_HEADS = 8
NA_HEAD_DIM = 64
NA_WIDTH = NA_HEADS * NA_HEAD_DIM
NA_WIN_R = 8
NA_WIN_C = 16
DA_HEADS = 4
DA_HEAD_DIM = 64
DA_V_DIM = 2 * DA_HEAD_DIM
DA_WIDTH = DA_HEADS * DA_V_DIM
MIX_WIDTH = NA_WIDTH + DA_WIDTH
IN_COLS = 4 * NA_WIDTH + 4 * DA_WIDTH
T5_BUCKETS = 32
T5_MAX_EXACT = 8
T5_MAX_DIST = 128
NORM_EPS = 1e-6
SUBLN_EPS = 1e-5

LANES = 128
SUBLANES = 8
NEG_BIG = -1e30
LOG2E = math.log2(math.e)

PROJ_BM = 512
PROJ_BN = 1024
NA_RB = 4
NA_KG = 3
NA_ONES_ROWS = 16
DA_BQ = 256
DA_BK = 256
DA_ONES_ROWS = 16
DA_TILE0 = -(-(T5_MAX_DIST + DA_BK) // DA_BQ)
DA_NTILES = DA_TILE0 + 2 + -(-T5_MAX_DIST // DA_BQ)
DA_NEAR = 3
VMEM_LIMIT = 48 * 1024 * 1024

_QA, _KA, _VA = 0, NA_WIDTH // LANES, 2 * NA_WIDTH // LANES
_QB = 4 * NA_WIDTH // LANES
_KB = _QB + DA_WIDTH // LANES
_VB = _KB + DA_WIDTH // LANES


def _norm_project(x, g_ref, w_ref, o_ref):
    ms = jnp.mean(x * x, axis=-1, keepdims=True)
    h = (x * lax.rsqrt(ms + NORM_EPS) * g_ref[...]).astype(BF16)
    for n in range(IN_COLS // PROJ_BN):
        cols = slice(n * PROJ_BN, (n + 1) * PROJ_BN)
        o_ref[:, cols] = jnp.dot(h, w_ref[:, cols], preferred_element_type=F32).astype(BF16)


def _inproj_kernel(x_ref, g_ref, w_ref, o_ref):
    _norm_project(x_ref[...], g_ref, w_ref, o_ref)


def _inproj(x2, g, w):
    m, d = x2.shape
    return pl.pallas_call(
        _inproj_kernel,
        grid=(m // PROJ_BM,),
        in_specs=[
            pl.BlockSpec((PROJ_BM, d), lambda i: (i, 0)),
            pl.BlockSpec((1, d), lambda i: (0, 0)),
            pl.BlockSpec((d, IN_COLS), lambda i: (0, 0)),
        ],
        out_specs=pl.BlockSpec((PROJ_BM, IN_COLS), lambda i: (i, 0)),
        out_shape=jax.ShapeDtypeStruct((m, IN_COLS), BF16),
        compiler_params=pltpu.CompilerParams(
            dimension_semantics=("arbitrary",), vmem_limit_bytes=VMEM_LIMIT),
        name="inproj",
    )(x2, g, w)


def _fold_logit_scale(w_in):
    col = np.ones((IN_COLS,), np.float32)
    col[0:NA_WIDTH] = NA_HEAD_DIM ** -0.5 * LOG2E
    col[4 * NA_WIDTH:4 * NA_WIDTH + DA_WIDTH] = DA_HEAD_DIM ** -0.5 * LOG2E
    return (w_in * col).astype(BF16)


def _na_block_table(kind):
    delta, rs_rel = {
        "first": (0, lambda x: 0),
        "mid": (-(NA_WIN_R // 2), lambda x: x - NA_WIN_R // 2),
        "last": (-NA_WIN_R, lambda x: NA_RB - NA_WIN_R),
    }[kind]
    table = {}
    for i in range(NA_KG * NA_RB):
        for x in range(NA_RB):
            if rs_rel(x) <= delta + i < rs_rel(x) + NA_WIN_R:
                table[(i, x)] = delta + i - x + (NA_WIN_R - 1)
    return table


def _na_kernel(q_ref, k_ref, v_ref, t_ref, o_ref, vt_scr, *, rows):
    ngroups = rows // NA_RB
    gk = NA_RB * GRID_W
    lo = lax.broadcasted_iota(jnp.int32, (1, LANES), 1) < NA_HEAD_DIM

    for g in range(ngroups):
        vt_scr[g, :LANES, :] = v_ref[0, g * gk:(g + 1) * gk, :].T
        vt_scr[g, LANES:, :] = jnp.ones((NA_ONES_ROWS, gk), BF16)

    def block(m, g0, kind):
        table = _na_block_table(kind)
        live = [g for g in range(NA_KG) if any((NA_RB * g + ii, x) in table
                                               for ii in range(NA_RB) for x in range(NA_RB))]
        qstart = pl.multiple_of(m * gk, gk)
        q = q_ref[0, pl.ds(qstart, gk), :]
        zero = jnp.zeros((GRID_W, LANES), q.dtype)
        pieces = []
        for x in range(NA_RB):
            qx = q[x * GRID_W:(x + 1) * GRID_W]
            pieces += [jnp.where(lo, qx, zero), jnp.where(lo, zero, qx)]
        qq_t = jnp.concatenate(pieces, axis=0).T
        kw = k_ref[0, pl.ds(pl.multiple_of((g0 + live[0]) * gk, gk), len(live) * gk), :]
        s_all = jnp.dot(kw, qq_t, preferred_element_type=F32)
        s = {g: s_all[n * gk:(n + 1) * gk] for n, g in enumerate(live)}
        p_blocks = {}
        for x in range(NA_RB):
            blks = {}
            for i in range(NA_KG * NA_RB):
                if (i, x) in table:
                    g, ii = divmod(i, NA_RB)
                    sb = s[g][ii * GRID_W:(ii + 1) * GRID_W, x * LANES:(x + 1) * LANES]
                    blks[i] = sb + t_ref[0, table[(i, x)]]
            mx = functools.reduce(jnp.maximum, blks.values())
            mx = jnp.max(mx, axis=0, keepdims=True)
            for i, sb in blks.items():
                p_blocks[(i, x)] = jnp.exp2(sb - mx)
        zero_blk = jnp.zeros((GRID_W, LANES), F32)
        p_rows = [jnp.concatenate([p_blocks.get((NA_RB * g + ii, x), zero_blk) for x in range(NA_RB)], axis=1)
                  for g in live for ii in range(NA_RB)]
        p_all = jnp.concatenate(p_rows, axis=0).astype(BF16)
        vt = jnp.concatenate([vt_scr[g0 + g] for g in live], axis=1)
        o_t = jnp.dot(vt, p_all, preferred_element_type=F32)
        inv_l = 1.0 / o_t[LANES:LANES + 1]
        o = (o_t[:LANES] * inv_l).T
        for x in range(NA_RB):
            top = o[x * LANES:x * LANES + GRID_W]
            bot = o[x * LANES + GRID_W:(x + 1) * LANES]
            o_ref[0, pl.ds(qstart + x * GRID_W, GRID_W), :] = jnp.where(lo, top, bot).astype(BF16)

    block(0, 0, "first")
    for m in range(1, ngroups - 1):
        block(m, m - 1, "mid")
    block(ngroups - 1, ngroups - NA_KG, "last")


def _na_attention(proj, na_table):
    b, s, _ = proj.shape
    rows = s // GRID_W
    assert rows % NA_RB == 0 and rows >= NA_KG * NA_RB
    npairs = NA_HEADS // 2
    blk = lambda off: pl.BlockSpec((1, s, LANES), lambda hp, bi: (bi, 0, off + hp))
    return pl.pallas_call(
        functools.partial(_na_kernel, rows=rows),
        grid=(npairs, b),
        in_specs=[
            blk(_QA), blk(_KA), blk(_VA),
            pl.BlockSpec((1, 2 * NA_WIN_R - 1, GRID_W, LANES), lambda hp, bi: (hp, 0, 0, 0)),
        ],
        out_specs=pl.BlockSpec((1, s, LANES), lambda hp, bi: (bi, 0, hp)),
        out_shape=jax.ShapeDtypeStruct((b, s, NA_WIDTH), BF16),
        scratch_shapes=[pltpu.VMEM((rows // NA_RB, LANES + NA_ONES_ROWS, NA_RB * GRID_W), BF16)],
        compiler_params=pltpu.CompilerParams(
            dimension_semantics=("arbitrary", "arbitrary"), vmem_limit_bytes=VMEM_LIMIT),
        name="na_attn",
    )(proj, proj, proj, na_table)


def _na_bias_tables(na_rpb):
    nl = na_rpb.shape[0]
    c = np.arange(GRID_W)[:, None]
    j = np.arange(GRID_W)[None, :]
    cs = np.clip(c - NA_WIN_C // 2, 0, GRID_W - NA_WIN_C)
    inside = (j >= cs) & (j < cs + NA_WIN_C)
    col_sel = ((j - c + (NA_WIN_C - 1))[:, :, None] == np.arange(2 * NA_WIN_C - 1)) & inside[:, :, None]
    rpb = (na_rpb * LOG2E).reshape(nl, NA_HEADS // 2, 2, 2 * NA_WIN_R - 1, 2 * NA_WIN_C - 1)
    t = jnp.einsum("lpeab,cjb->lpajec", rpb, col_sel.astype(np.float32), precision=lax.Precision.HIGHEST)
    t = t + np.where(inside, 0.0, NEG_BIG).astype(np.float32).T[:, None, :]
    return t.reshape(nl, NA_HEADS // 2, 2 * NA_WIN_R - 1, GRID_W, LANES)


def _da_kernel(lamv_ref, q_ref, k_ref, v_ref, u_ref, g_ref, o_ref, vt_scr, s_even, s_odd, tiles_scr,
               *, seq, lam_init):
    nchunks = seq // DA_BK
    nq = seq // DA_BQ
    lo = lax.broadcasted_iota(jnp.int32, (1, LANES), 1) < DA_HEAD_DIM
    nsub = DA_BK // SUBLANES

    lv = lamv_ref[...]
    lam = (jnp.exp(jnp.sum(lv[0:1] * lv[1:2], axis=-1, keepdims=True))
           - jnp.exp(jnp.sum(lv[2:3] * lv[3:4], axis=-1, keepdims=True)) + lam_init)

    for c in range(nchunks):
        vt_scr[c, :DA_V_DIM, :] = v_ref[0, c * DA_BK:(c + 1) * DA_BK, :].T
        vt_scr[c, DA_V_DIM:, :] = jnp.ones((DA_ONES_ROWS, DA_BK), BF16)

    @pl.when(pl.program_id(1) == 0)
    def _():
        for t in range(DA_NTILES):
            rows = jnp.broadcast_to(u_ref[0, t], (DA_BK, DA_BQ + DA_BK))
            tiles_scr[t] = pltpu.roll(rows, 0, 1, stride=1, stride_axis=0)[:, :DA_BQ]

    delta = u_ref[0, DA_NTILES - 1, 0:1, 0:1]
    ratio = DA_BK // DA_BQ

    def step(qi_next, s_next, qi, s_cur, m_cur, qi_done, acc_done):
        if qi_done is not None:
            a1, a2 = acc_done
            l1 = a1[DA_V_DIM:DA_V_DIM + 1]
            l2 = a2[DA_V_DIM:DA_V_DIM + 1]
            o = a1[:DA_V_DIM] / l1 - lam * (a2[:DA_V_DIM] / l2)
            ms = jnp.mean(o * o, axis=0, keepdims=True)
            y = o * lax.rsqrt(ms + SUBLN_EPS) * g_ref[...]
            y = y * (1.0 - lam_init)
            o_ref[0, pl.ds(pl.multiple_of(qi_done * DA_BQ, DA_BQ), DA_BQ), :] = y.T.astype(BF16)
        if qi_next is not None:
            q = q_ref[0, pl.ds(pl.multiple_of(qi_next * DA_BQ, DA_BQ), DA_BQ), :]
            zero = jnp.zeros_like(q)
            q_t = jnp.concatenate([jnp.where(lo, q, zero), jnp.where(lo, zero, q)], axis=0).T
            own_next = qi_next // ratio
        if qi is not None:
            own = qi // ratio
        m8 = None
        a1 = jnp.zeros((DA_V_DIM + DA_ONES_ROWS, DA_BQ), F32)
        a2 = jnp.zeros((DA_V_DIM + DA_ONES_ROWS, DA_BQ), F32)
        for c in range(nchunks):
            if qi_next is not None:
                cl = lax.rem(own_next + (nchunks - 1) + c, nchunks)
                kc = k_ref[0, pl.ds(pl.multiple_of(cl * DA_BK, DA_BK), DA_BK), :]
                s = jnp.dot(kc, q_t, preferred_element_type=F32)
                if c < DA_NEAR:
                    bt = tiles_scr[jnp.clip(ratio * cl - qi_next + DA_TILE0, 0, DA_NTILES - 1)]
                    s = s + jnp.concatenate([bt, bt], axis=1)
                s_next[cl] = s
                cm = jnp.max(s.reshape(nsub, SUBLANES, 2 * DA_BQ), axis=0)
                if c >= DA_NEAR:
                    cm = cm + jnp.where(own_next + c <= nchunks, delta, 0.0)
                m8 = cm if m8 is None else jnp.maximum(m8, cm)
            if qi is not None:
                far = lax.rem(c - own + 1 + nchunks, nchunks) >= DA_NEAR
                m_c = m_cur - jnp.where(jnp.logical_and(far, c > own), delta, 0.0)
                pb = jnp.exp2(s_cur[c] - m_c).astype(BF16)
                vt = vt_scr[c]
                a1 = a1 + jnp.dot(vt, pb[:, :DA_BQ], preferred_element_type=F32)
                a2 = a2 + jnp.dot(vt, pb[:, DA_BQ:], preferred_element_type=F32)
        return (None if m8 is None else jnp.max(m8, axis=0, keepdims=True)), (a1, a2)

    m0, _ = step(0, s_even, None, None, None, None, None)
    m1, acc0 = step(1, s_odd, 0, s_even, m0, None, None)

    def pair(jj, carry):
        m_prev, acc_prev = carry
        k = 2 * jj
        m_k, acc_km1 = step(k, s_even, k - 1, s_odd, m_prev, k - 2, acc_prev)
        return step(k + 1, s_odd, k, s_even, m_k, k - 1, acc_km1)

    m_last, acc_prev = lax.fori_loop(1, nq // 2, pair, (m1, acc0))
    _, acc_last = step(None, None, nq - 1, s_odd, m_last, nq - 2, acc_prev)
    step(None, None, None, None, None, nq - 1, acc_last)


def _da_attention(proj, lamv, bias_rows, subln_col, lam_init):
    b, s, _ = proj.shape
    assert (s // DA_BQ) % 2 == 0 and s // DA_BK >= DA_NEAR
    blk = lambda off: pl.BlockSpec((1, s, LANES), lambda h, bi: (bi, 0, off + h))
    logits_scratch = pltpu.VMEM((s // DA_BK, DA_BK, 2 * DA_BQ), F32)
    return pl.pallas_call(
        functools.partial(_da_kernel, seq=s, lam_init=lam_init),
        grid=(DA_HEADS, b),
        in_specs=[
            pl.BlockSpec((4, DA_HEAD_DIM), lambda h, bi: (0, 0)),
            blk(_QB), blk(_KB), blk(_VB),
            pl.BlockSpec((1, DA_NTILES, 1, DA_BQ + DA_BK), lambda h, bi: (h, 0, 0, 0)),
            pl.BlockSpec((DA_V_DIM, 1), lambda h, bi: (0, 0)),
        ],
        out_specs=pl.BlockSpec((1, s, LANES), lambda h, bi: (bi, 0, h)),
        out_shape=jax.ShapeDtypeStruct((b, s, DA_WIDTH), BF16),
        scratch_shapes=[
            pltpu.VMEM((s // DA_BK, DA_V_DIM + DA_ONES_ROWS, DA_BK), BF16),
            logits_scratch,
            logits_scratch,
            pltpu.VMEM((DA_NTILES, DA_BK, DA_BQ), F32),
        ],
        compiler_params=pltpu.CompilerParams(
            dimension_semantics=("arbitrary", "arbitrary"), vmem_limit_bytes=VMEM_LIMIT),
        name="da_attn",
    )(lamv, proj, proj, proj, bias_rows, subln_col)


def _t5_bucket(rel):
    n = T5_BUCKETS // 2
    ret = jnp.where(rel > 0, n, 0)
    a = jnp.abs(rel)
    small = a < T5_MAX_EXACT
    af = jnp.maximum(a, 1).astype(jnp.float32)
    large = T5_MAX_EXACT + (jnp.log(af / T5_MAX_EXACT) / math.log(T5_MAX_DIST / T5_MAX_EXACT)
                            * (n - T5_MAX_EXACT)).astype(jnp.int32)
    large = jnp.minimum(large, n - 1)
    return ret + jnp.where(small, a, large)


def _da_bias_rows(t5_table):
    period = DA_BQ + DA_BK
    m = np.arange(period)
    rel = np.where(m <= DA_BQ, -m, period - m)
    off = DA_BQ * (np.arange(DA_NTILES) - DA_TILE0)
    u = t5_table[_t5_bucket(jnp.asarray(off[:, None] + rel[None, :], jnp.int32))]
    u = (u * LOG2E).astype(F32).transpose(2, 0, 1)
    u = u - u[:, :1, :1]
    return u.reshape(DA_HEADS, DA_NTILES, 1, period)


def _silu(g):
    return g / (1.0 + jnp.exp(-g))


def _mix_residual(oa_ref, ob_ref, ga_ref, gb_ref, x_ref, w_ref):
    ya = (oa_ref[...].astype(F32) * _silu(ga_ref[...].astype(F32))).astype(BF16)
    yb = (ob_ref[...].astype(F32) * _silu(gb_ref[...].astype(F32))).astype(BF16)
    acc = jnp.dot(ya, w_ref[:NA_WIDTH, :], preferred_element_type=F32)
    acc = acc + jnp.dot(yb, w_ref[NA_WIDTH:, :], preferred_element_type=F32)
    return x_ref[...] + acc


def _outproj_final_kernel(oa_ref, ob_ref, ga_ref, gb_ref, x_ref, w_ref, fg_ref, o_ref):
    xn = _mix_residual(oa_ref, ob_ref, ga_ref, gb_ref, x_ref, w_ref)
    ms = jnp.mean(xn * xn, axis=-1, keepdims=True)
    o_ref[...] = xn * lax.rsqrt(ms + NORM_EPS) * fg_ref[...]


def _outproj_inproj_kernel(oa_ref, ob_ref, ga_ref, gb_ref, x_ref, wo_ref, g_ref, wi_ref, xo_ref, p_ref):
    xn = _mix_residual(oa_ref, ob_ref, ga_ref, gb_ref, x_ref, wo_ref)
    xo_ref[...] = xn
    _norm_project(xn, g_ref, wi_ref, p_ref)


def _mix_in_specs(d):
    ga_blk = 3 * NA_WIDTH // NA_WIDTH
    gb_blk = (4 * NA_WIDTH + 3 * DA_WIDTH) // DA_WIDTH
    return [
        pl.BlockSpec((PROJ_BM, NA_WIDTH), lambda i: (i, 0)),
        pl.BlockSpec((PROJ_BM, DA_WIDTH), lambda i: (i, 0)),
        pl.BlockSpec((PROJ_BM, NA_WIDTH), lambda i: (i, ga_blk)),
        pl.BlockSpec((PROJ_BM, DA_WIDTH), lambda i: (i, gb_blk)),
        pl.BlockSpec((PROJ_BM, d), lambda i: (i, 0)),
        pl.BlockSpec((MIX_WIDTH, d), lambda i: (0, 0), pipeline_mode=pl.Buffered(1)),
    ]


def _outproj_final(out_a, out_b, proj2, x2, w, final_g):
    m, d = x2.shape
    return pl.pallas_call(
        _outproj_final_kernel,
        grid=(m // PROJ_BM,),
        in_specs=_mix_in_specs(d) + [pl.BlockSpec((1, d), lambda i: (0, 0))],
        out_specs=pl.BlockSpec((PROJ_BM, d), lambda i: (i, 0)),
        out_shape=jax.ShapeDtypeStruct((m, d), F32),
        input_output_aliases={4: 0},
        compiler_params=pltpu.CompilerParams(
            dimension_semantics=("arbitrary",), vmem_limit_bytes=VMEM_LIMIT),
        name="outproj_final",
    )(out_a, out_b, proj2, proj2, x2, w, final_g)


def _outproj_inproj(out_a, out_b, proj2, x2, w_out, g, w_in, in_place):
    m, d = x2.shape
    return pl.pallas_call(
        _outproj_inproj_kernel,
        grid=(m // PROJ_BM,),
        in_specs=_mix_in_specs(d) + [
            pl.BlockSpec((1, d), lambda i: (0, 0)),
            pl.BlockSpec((d, IN_COLS), lambda i: (0, 0), pipeline_mode=pl.Buffered(1)),
        ],
        out_specs=[
            pl.BlockSpec((PROJ_BM, d), lambda i: (i, 0)),
            pl.BlockSpec((PROJ_BM, IN_COLS), lambda i: (i, 0)),
        ],
        out_shape=[jax.ShapeDtypeStruct((m, d), F32), jax.ShapeDtypeStruct((m, IN_COLS), BF16)],
        input_output_aliases={4: 0} if in_place else {},
        compiler_params=pltpu.CompilerParams(
            dimension_semantics=("arbitrary",), vmem_limit_bytes=VMEM_LIMIT),
        name="outproj_inproj",
    )(out_a, out_b, proj2, proj2, x2, w_out, g, w_in)


def kernel(x, norm_g, w_in, na_rpb, lambda_q1, lambda_k1, lambda_q2, lambda_k2, subln_g, t5_table, w_out, final_g):
    b, s, d = x.shape
    depth = w_in.shape[0]
    assert s % DA_BK == 0 and s % GRID_W == 0 and (b * s) % PROJ_BM == 0

    w_in_b = _fold_logit_scale(w_in.astype(F32))
    w_out_b = w_out.astype(BF16)
    na_bias = _na_bias_tables(na_rpb.astype(F32))
    da_bias = _da_bias_rows(t5_table.astype(F32))
    lamv = jnp.stack([lambda_q1, lambda_k1, lambda_q2, lambda_k2], axis=1).astype(F32)
    norm_rows = norm_g.reshape(depth, 1, d).astype(F32)

    x2 = x.reshape(b * s, d)
    proj2 = _inproj(x2, norm_rows[0], w_in_b[0])
    for l in range(depth):
        lam_init = 0.8 - 0.6 * math.exp(-0.3 * l)
        proj = proj2.reshape(b, s, IN_COLS)
        out_a = _na_attention(proj, na_bias[l]).reshape(b * s, NA_WIDTH)
        out_b = _da_attention(proj, lamv[l], da_bias, subln_g[l].reshape(DA_V_DIM, 1).astype(F32), lam_init)
        out_b = out_b.reshape(b * s, DA_WIDTH)
        if l + 1 < depth:
            x2, proj2 = _outproj_inproj(out_a, out_b, proj2, x2, w_out_b[l], norm_rows[l + 1], w_in_b[l + 1],
                                        in_place=(l > 0))
        else:
            x2 = _outproj_final(out_a, out_b, proj2, x2, w_out_b[l], final_g.reshape(1, d).astype(F32))
    return x2.reshape(b, s, d)
```

```python
import functools
import math

import jax
import jax.numpy as jnp
import numpy as np
from jax import lax
from jax.experimental import pallas as pl
from jax.experimental.pallas import tpu as pltpu

F32 = jnp.float32
BF16 = jnp.bfloat16

GRID_W = 64
NA_HEADS = 8
NA_HEAD_DIM = 64
NA_WIDTH = NA_HEADS * NA_HEAD_DIM
NA_WIN_R = 8
NA_WIN_C = 16
DA_HEADS = 4
DA_HEAD_DIM = 64
DA_V_DIM = 2 * DA_HEAD_DIM
DA_WIDTH = DA_HEADS * DA_V_DIM
MIX_WIDTH = NA_WIDTH + DA_WIDTH
IN_COLS = 4 * NA_WIDTH + 4 * DA_WIDTH
T5_BUCKETS = 32
T5_MAX_EXACT = 8
T5_MAX_DIST = 128
NORM_EPS = 1e-6
SUBLN_EPS = 1e-5

LANES = 128
SUBLANES = 8
NEG_BIG = -1e30
LOG2E = math.log2(math.e)

PROJ_BM = 512
PROJ_BN = 1024
NA_RB = 4
NA_KG = 3
NA_ONES_ROWS = 16
DA_BQ = 256
DA_BK = 256
DA_ONES_ROWS = 16
DA_TILE0 = -(-(T5_MAX_DIST + DA_BK) // DA_BQ)
DA_NTILES = DA_TILE0 + 2 + -(-T5_MAX_DIST // DA_BQ)
DA_NEAR = 3
VMEM_LIMIT = 48 * 1024 * 1024

_QA, _KA, _VA = 0, NA_WIDTH // LANES, 2 * NA_WIDTH // LANES
_QB = 4 * NA_WIDTH // LANES
_KB = _QB + DA_WIDTH // LANES
_VB = _KB + DA_WIDTH // LANES


def _norm_project(x, g_ref, w_ref, o_ref):
    ms = jnp.mean(x * x, axis=-1, keepdims=True)
    h = (x * lax.rsqrt(ms + NORM_EPS) * g_ref[...]).astype(BF16)
    for n in range(IN_COLS // PROJ_BN):
        cols = slice(n * PROJ_BN, (n + 1) * PROJ_BN)
        o_ref[:, cols] = jnp.dot(h, w_ref[:, cols], preferred_element_type=F32).astype(BF16)


def _inproj_kernel(x_ref, g_ref, w_ref, o_ref):
    _norm_project(x_ref[...], g_ref, w_ref, o_ref)


def _inproj(x2, g, w):
    m, d = x2.shape
    return pl.pallas_call(
        _inproj_kernel,
        grid=(m // PROJ_BM,),
        in_specs=[
            pl.BlockSpec((PROJ_BM, d), lambda i: (i, 0)),
            pl.BlockSpec((1, d), lambda i: (0, 0)),
            pl.BlockSpec((d, IN_COLS), lambda i: (0, 0)),
        ],
        out_specs=pl.BlockSpec((PROJ_BM, IN_COLS), lambda i: (i, 0)),
        out_shape=jax.ShapeDtypeStruct((m, IN_COLS), BF16),
        compiler_params=pltpu.CompilerParams(
            dimension_semantics=("arbitrary",), vmem_limit_bytes=VMEM_LIMIT),
        name="inproj",
    )(x2, g, w)


def _fold_logit_scale(w_in):
    col = np.ones((IN_COLS,), np.float32)
    col[0:NA_WIDTH] = NA_HEAD_DIM ** -0.5 * LOG2E
    col[4 * NA_WIDTH:4 * NA_WIDTH + DA_WIDTH] = DA_HEAD_DIM ** -0.5 * LOG2E
    return (w_in * col).astype(BF16)


def _na_block_table(kind):
    delta, rs_rel = {
        "first": (0, lambda x: 0),
        "mid": (-(NA_WIN_R // 2), lambda x: x - NA_WIN_R // 2),
        "last": (-NA_WIN_R, lambda x: NA_RB - NA_WIN_R),
    }[kind]
    table = {}
    for i in range(NA_KG * NA_RB):
        for x in range(NA_RB):
            if rs_rel(x) <= delta + i < rs_rel(x) + NA_WIN_R:
                table[(i, x)] = delta + i - x + (NA_WIN_R - 1)
    return table


def _na_kernel(q_ref, k_ref, v_ref, t_ref, o_ref, vt_scr, qt_scr, *, rows):
    ngroups = rows // NA_RB
    gk = NA_RB * GRID_W
    lo = lax.broadcasted_iota(jnp.int32, (1, LANES), 1) < NA_HEAD_DIM

    for g in range(ngroups):
        vt_scr[g, :LANES, :] = v_ref[0, g * gk:(g + 1) * gk, :].T
        vt_scr[g, LANES:, :] = jnp.ones((NA_ONES_ROWS, gk), BF16)

    for m in range(ngroups):
        q = q_ref[0, m * gk:(m + 1) * gk, :]
        zero = jnp.zeros((GRID_W, LANES), q.dtype)
        pieces = []
        for x in range(NA_RB):
            qx = q[x * GRID_W:(x + 1) * GRID_W]
            pieces += [jnp.where(lo, qx, zero), jnp.where(lo, zero, qx)]
        qt_scr[m] = jnp.concatenate(pieces, axis=0).T

    def block(m, g0, kind):
        table = _na_block_table(kind)
        live = [g for g in range(NA_KG) if any((NA_RB * g + ii, x) in table
                                               for ii in range(NA_RB) for x in range(NA_RB))]
        qstart = pl.multiple_of(m * gk, gk)
        qq_t = qt_scr[m]
        kw = k_ref[0, pl.ds(pl.multiple_of((g0 + live[0]) * gk, gk), len(live) * gk), :]
        s_all = jnp.dot(kw, qq_t, preferred_element_type=F32)
        s = {g: s_all[n * gk:(n + 1) * gk] for n, g in enumerate(live)}
        p_blocks = {}
        for x in range(NA_RB):
            blks = {}
            for i in range(NA_KG * NA_RB):
                if (i, x) in table:
                    g, ii = divmod(i, NA_RB)
                    sb = s[g][ii * GRID_W:(ii + 1) * GRID_W, x * LANES:(x + 1) * LANES]
                    blks[i] = sb + t_ref[0, table[(i, x)]]
            mx = functools.reduce(jnp.maximum, blks.values())
            mx = jnp.max(mx, axis=0, keepdims=True)
            for i, sb in blks.items():
                p_blocks[(i, x)] = jnp.exp2(sb - mx)
        zero_blk = jnp.zeros((GRID_W, LANES), F32)
        p_rows = [jnp.concatenate([p_blocks.get((NA_RB * g + ii, x), zero_blk) for x in range(NA_RB)], axis=1)
                  for g in live for ii in range(NA_RB)]
        p_all = jnp.concatenate(p_rows, axis=0).astype(BF16)
        vt = jnp.concatenate([vt_scr[g0 + g] for g in live], axis=1)
        o_t = jnp.dot(vt, p_all, preferred_element_type=F32)
        inv_l = 1.0 / o_t[LANES:LANES + 1]
        o = (o_t[:LANES] * inv_l).T
        for x in range(NA_RB):
            top = o[x * LANES:x * LANES + GRID_W]
            bot = o[x * LANES + GRID_W:(x + 1) * LANES]
            o_ref[0, pl.ds(qstart + x * GRID_W, GRID_W), :] = jnp.where(lo, top, bot).astype(BF16)

    block(0, 0, "first")
    for m in range(1, ngroups - 1):
        block(m, m - 1, "mid")
    block(ngroups - 1, ngroups - NA_KG, "last")


def _na_attention(proj, na_table):
    b, s, _ = proj.shape
    rows = s // GRID_W
    assert rows % NA_RB == 0 and rows >= NA_KG * NA_RB
    npairs = NA_HEADS // 2
    blk = lambda off: pl.BlockSpec((1, s, LANES), lambda hp, bi: (bi, 0, off + hp))
    return pl.pallas_call(
        functools.partial(_na_kernel, rows=rows),
        grid=(npairs, b),
        in_specs=[
            blk(_QA), blk(_KA), blk(_VA),
            pl.BlockSpec((1, 2 * NA_WIN_R - 1, GRID_W, LANES), lambda hp, bi: (hp, 0, 0, 0)),
        ],
        out_specs=pl.BlockSpec((1, s, LANES), lambda hp, bi: (bi, 0, hp)),
        out_shape=jax.ShapeDtypeStruct((b, s, NA_WIDTH), BF16),
        scratch_shapes=[pltpu.VMEM((rows // NA_RB, LANES + NA_ONES_ROWS, NA_RB * GRID_W), BF16),
                        pltpu.VMEM((rows // NA_RB, LANES, NA_RB * LANES), BF16)],
        compiler_params=pltpu.CompilerParams(
            dimension_semantics=("arbitrary", "arbitrary"), vmem_limit_bytes=VMEM_LIMIT),
        name="na_attn",
    )(proj, proj, proj, na_table)


def _na_bias_tables(na_rpb):
    nl = na_rpb.shape[0]
    c = np.arange(GRID_W)[:, None]
    j = np.arange(GRID_W)[None, :]
    cs = np.clip(c - NA_WIN_C // 2, 0, GRID_W - NA_WIN_C)
    inside = (j >= cs) & (j < cs + NA_WIN_C)
    col_sel = ((j - c + (NA_WIN_C - 1))[:, :, None] == np.arange(2 * NA_WIN_C - 1)) & inside[:, :, None]
    rpb = (na_rpb * LOG2E).reshape(nl, NA_HEADS // 2, 2, 2 * NA_WIN_R - 1, 2 * NA_WIN_C - 1)
    t = jnp.einsum("lpeab,cjb->lpajec", rpb, col_sel.astype(np.float32), precision=lax.Precision.HIGHEST)
    t = t + np.where(inside, 0.0, NEG_BIG).astype(np.float32).T[:, None, :]
    return t.reshape(nl, NA_HEADS // 2, 2 * NA_WIN_R - 1, GRID_W, LANES)


def _da_kernel(lamv_ref, q_ref, k_ref, v_ref, u_ref, g_ref, o_ref, vt_scr, qt_scr, s_even, s_odd, tiles_scr,
               *, seq, lam_init):
    nchunks = seq // DA_BK
    nq = seq // DA_BQ
    lo = lax.broadcasted_iota(jnp.int32, (1, LANES), 1) < DA_HEAD_DIM
    nsub = DA_BK // SUBLANES

    lv = lamv_ref[...]
    lam = (jnp.exp(jnp.sum(lv[0:1] * lv[1:2], axis=-1, keepdims=True))
           - jnp.exp(jnp.sum(lv[2:3] * lv[3:4], axis=-1, keepdims=True)) + lam_init)

    for c in range(nchunks):
        vt_scr[c, :DA_V_DIM, :] = v_ref[0, c * DA_BK:(c + 1) * DA_BK, :].T
        vt_scr[c, DA_V_DIM:, :] = jnp.ones((DA_ONES_ROWS, DA_BK), BF16)

    for qb in range(nq):
        q = q_ref[0, qb * DA_BQ:(qb + 1) * DA_BQ, :]
        zero = jnp.zeros_like(q)
        qt_scr[qb] = jnp.concatenate([jnp.where(lo, q, zero), jnp.where(lo, zero, q)], axis=0).T

    @pl.when(pl.program_id(1) == 0)
    def _():
        for t in range(DA_NTILES):
            rows = jnp.broadcast_to(u_ref[0, t], (DA_BK, DA_BQ + DA_BK))
            tiles_scr[t] = pltpu.roll(rows, 0, 1, stride=1, stride_axis=0)[:, :DA_BQ]

    delta = u_ref[0, DA_NTILES - 1, 0:1, 0:1]
    ratio = DA_BK // DA_BQ

    def step(qi_next, s_next, qi, s_cur, m_cur, qi_done, acc_done):
        if qi_done is not None:
            a1, a2 = acc_done
            l1 = a1[DA_V_DIM:DA_V_DIM + 1]
            l2 = a2[DA_V_DIM:DA_V_DIM + 1]
            o = a1[:DA_V_DIM] / l1 - lam * (a2[:DA_V_DIM] / l2)
            ms = jnp.mean(o * o, axis=0, keepdims=True)
            y = o * lax.rsqrt(ms + SUBLN_EPS) * g_ref[...]
            y = y * (1.0 - lam_init)
            o_ref[0, pl.ds(pl.multiple_of(qi_done * DA_BQ, DA_BQ), DA_BQ), :] = y.T.astype(BF16)
        if qi_next is not None:
            q_t = qt_scr[qi_next]
            own_next = qi_next // ratio
        if qi is not None:
            own = qi // ratio
        m8 = None
        a1 = jnp.zeros((DA_V_DIM + DA_ONES_ROWS, DA_BQ), F32)
        a2 = jnp.zeros((DA_V_DIM + DA_ONES_ROWS, DA_BQ), F32)
        for c in range(nchunks):
            if qi_next is not None:
                cl = lax.rem(own_next + (nchunks - 1) + c, nchunks)
                kc = k_ref[0, pl.ds(pl.multiple_of(cl * DA_BK, DA_BK), DA_BK), :]
                s = jnp.dot(kc, q_t, preferred_element_type=F32)
                if c < DA_NEAR:
                    bt = tiles_scr[jnp.clip(ratio * cl - qi_next + DA_TILE0, 0, DA_NTILES - 1)]
                    s = s + jnp.concatenate([bt, bt], axis=1)
                s_next[cl] = s
                cm = jnp.max(s.reshape(nsub, SUBLANES, 2 * DA_BQ), axis=0)
                if c >= DA_NEAR:
                    cm = cm + jnp.where(own_next + c <= nchunks, delta, 0.0)
                m8 = cm if m8 is None else jnp.maximum(m8, cm)
            if qi is not None:
                far = lax.rem(c - own + 1 + nchunks, nchunks) >= DA_NEAR
                m_c = m_cur - jnp.where(jnp.logical_and(far, c > own), delta, 0.0)
                pb = jnp.exp2(s_cur[c] - m_c).astype(BF16)
                vt = vt_scr[c]
                a1 = a1 + jnp.dot(vt, pb[:, :DA_BQ], preferred_element_type=F32)
                a2 = a2 + jnp.dot(vt, pb[:, DA_BQ:], preferred_element_type=F32)
        return (None if m8 is None else jnp.max(m8, axis=0, keepdims=True)), (a1, a2)

    m0, _ = step(0, s_even, None, None, None, None, None)
    m1, acc0 = step(1, s_odd, 0, s_even, m0, None, None)

    def pair(jj, carry):
        m_prev, acc_prev = carry
        k = 2 * jj
        m_k, acc_km1 = step(k, s_even, k - 1, s_odd, m_prev, k - 2, acc_prev)
        return step(k + 1, s_odd, k, s_even, m_k, k - 1, acc_km1)

    m_last, acc_prev = lax.fori_loop(1, nq // 2, pair, (m1, acc0))
    _, acc_last = step(None, None, nq - 1, s_odd, m_last, nq - 2, acc_prev)
    step(None, None, None, None, None, nq - 1, acc_last)


def _da_attention(proj, lamv, bias_rows, subln_col, lam_init):
    b, s, _ = proj.shape
    assert (s // DA_BQ) % 2 == 0 and s // DA_BK >= DA_NEAR
    blk = lambda off: pl.BlockSpec((1, s, LANES), lambda h, bi: (bi, 0, off + h))
    logits_scratch = pltpu.VMEM((s // DA_BK, DA_BK, 2 * DA_BQ), F32)
    return pl.pallas_call(
        functools.partial(_da_kernel, seq=s, lam_init=lam_init),
        grid=(DA_HEADS, b),
        in_specs=[
            pl.BlockSpec((4, DA_HEAD_DIM), lambda h, bi: (0, 0)),
            blk(_QB), blk(_KB), blk(_VB),
            pl.BlockSpec((1, DA_NTILES, 1, DA_BQ + DA_BK), lambda h, bi: (h, 0, 0, 0)),
            pl.BlockSpec((DA_V_DIM, 1), lambda h, bi: (0, 0)),
        ],
        out_specs=pl.BlockSpec((1, s, LANES), lambda h, bi: (bi, 0, h)),
        out_shape=jax.ShapeDtypeStruct((b, s, DA_WIDTH), BF16),
        scratch_shapes=[
            pltpu.VMEM((s // DA_BK, DA_V_DIM + DA_ONES_ROWS, DA_BK), BF16),
            pltpu.VMEM((s // DA_BQ, LANES, 2 * DA_BQ), BF16),
            logits_scratch,
            logits_scratch,
            pltpu.VMEM((DA_NTILES, DA_BK, DA_BQ), F32),
        ],
        compiler_params=pltpu.CompilerParams(
            dimension_semantics=("arbitrary", "arbitrary"), vmem_limit_bytes=VMEM_LIMIT),
        name="da_attn",
    )(lamv, proj, proj, proj, bias_rows, subln_col)


def _t5_bucket(rel):
    n = T5_BUCKETS // 2
    ret = jnp.where(rel > 0, n, 0)
    a = jnp.abs(rel)
    small = a < T5_MAX_EXACT
    af = jnp.maximum(a, 1).astype(jnp.float32)
    large = T5_MAX_EXACT + (jnp.log(af / T5_MAX_EXACT) / math.log(T5_MAX_DIST / T5_MAX_EXACT)
                            * (n - T5_MAX_EXACT)).astype(jnp.int32)
    large = jnp.minimum(large, n - 1)
    return ret + jnp.where(small, a, large)


def _da_bias_rows(t5_table):
    period = DA_BQ + DA_BK
    m = np.arange(period)
    rel = np.where(m <= DA_BQ, -m, period - m)
    off = DA_BQ * (np.arange(DA_NTILES) - DA_TILE0)
    u = t5_table[_t5_bucket(jnp.asarray(off[:, None] + rel[None, :], jnp.int32))]
    u = (u * LOG2E).astype(F32).transpose(2, 0, 1)
    u = u - u[:, :1, :1]
    return u.reshape(DA_HEADS, DA_NTILES, 1, period)


def _silu(g):
    return g / (1.0 + jnp.exp(-g))


def _mix_residual(oa_ref, ob_ref, ga_ref, gb_ref, x_ref, w_ref):
    ya = (oa_ref[...].astype(F32) * _silu(ga_ref[...].astype(F32))).astype(BF16)
    yb = (ob_ref[...].astype(F32) * _silu(gb_ref[...].astype(F32))).astype(BF16)
    acc = jnp.dot(ya, w_ref[:NA_WIDTH, :], preferred_element_type=F32)
    acc = acc + jnp.dot(yb, w_ref[NA_WIDTH:, :], preferred_element_type=F32)
    return x_ref[...] + acc


def _outproj_final_kernel(oa_ref, ob_ref, ga_ref, gb_ref, x_ref, w_ref, fg_ref, o_ref):
    xn = _mix_residual(oa_ref, ob_ref, ga_ref, gb_ref, x_ref, w_ref)
    ms = jnp.mean(xn * xn, axis=-1, keepdims=True)
    o_ref[...] = xn * lax.rsqrt(ms + NORM_EPS) * fg_ref[...]


def _outproj_inproj_kernel(oa_ref, ob_ref, ga_ref, gb_ref, x_ref, wo_ref, g_ref, wi_ref, xo_ref, p_ref):
    xn = _mix_residual(oa_ref, ob_ref, ga_ref, gb_ref, x_ref, wo_ref)
    xo_ref[...] = xn
    _norm_project(xn, g_ref, wi_ref, p_ref)


def _mix_in_specs(d):
    ga_blk = 3 * NA_WIDTH // NA_WIDTH
    gb_blk = (4 * NA_WIDTH + 3 * DA_WIDTH) // DA_WIDTH
    return [
        pl.BlockSpec((PROJ_BM, NA_WIDTH), lambda i: (i, 0)),
        pl.BlockSpec((PROJ_BM, DA_WIDTH), lambda i: (i, 0)),
        pl.BlockSpec((PROJ_BM, NA_WIDTH), lambda i: (i, ga_blk)),
        pl.BlockSpec((PROJ_BM, DA_WIDTH), lambda i: (i, gb_blk)),
        pl.BlockSpec((PROJ_BM, d), lambda i: (i, 0)),
        pl.BlockSpec((MIX_WIDTH, d), lambda i: (0, 0), pipeline_mode=pl.Buffered(1)),
    ]


def _outproj_final(out_a, out_b, proj2, x2, w, final_g):
    m, d = x2.shape
    return pl.pallas_call(
        _outproj_final_kernel,
        grid=(m // PROJ_BM,),
        in_specs=_mix_in_specs(d) + [pl.BlockSpec((1, d), lambda i: (0, 0))],
        out_specs=pl.BlockSpec((PROJ_BM, d), lambda i: (i, 0)),
        out_shape=jax.ShapeDtypeStruct((m, d), F32),
        input_output_aliases={4: 0},
        compiler_params=pltpu.CompilerParams(
            dimension_semantics=("arbitrary",), vmem_limit_bytes=VMEM_LIMIT),
        name="outproj_final",
    )(out_a, out_b, proj2, proj2, x2, w, final_g)


def _outproj_inproj(out_a, out_b, proj2, x2, w_out, g, w_in, in_place):
    m, d = x2.shape
    return pl.pallas_call(
        _outproj_inproj_kernel,
        grid=(m // PROJ_BM,),
        in_specs=_mix_in_specs(d) + [
            pl.BlockSpec((1, d), lambda i: (0, 0)),
            pl.BlockSpec((d, IN_COLS), lambda i: (0, 0), pipeline_mode=pl.Buffered(1)),
        ],
        out_specs=[
            pl.BlockSpec((PROJ_BM, d), lambda i: (i, 0)),
            pl.BlockSpec((PROJ_BM, IN_COLS), lambda i: (i, 0)),
        ],
        out_shape=[jax.ShapeDtypeStruct((m, d), F32), jax.ShapeDtypeStruct((m, IN_COLS), BF16)],
        input_output_aliases={4: 0} if in_place else {},
        compiler_params=pltpu.CompilerParams(
            dimension_semantics=("arbitrary",), vmem_limit_bytes=VMEM_LIMIT),
        name="outproj_inproj",
    )(out_a, out_b, proj2, proj2, x2, w_out, g, w_in)


def kernel(x, norm_g, w_in, na_rpb, lambda_q1, lambda_k1, lambda_q2, lambda_k2, subln_g, t5_table, w_out, final_g):
    b, s, d = x.shape
    depth = w_in.shape[0]
    assert s % DA_BK == 0 and s % GRID_W == 0 and (b * s) % PROJ_BM == 0

    w_in_b = _fold_logit_scale(w_in.astype(F32))
    w_out_b = w_out.astype(BF16)
    na_bias = _na_bias_tables(na_rpb.astype(F32))
    da_bias = _da_bias_rows(t5_table.astype(F32))
    lamv = jnp.stack([lambda_q1, lambda_k1, lambda_q2, lambda_k2], axis=1).astype(F32)
    norm_rows = norm_g.reshape(depth, 1, d).astype(F32)

    x2 = x.reshape(b * s, d)
    proj2 = _inproj(x2, norm_rows[0], w_in_b[0])
    for l in range(depth):
        lam_init = 0.8 - 0.6 * math.exp(-0.3 * l)
        proj = proj2.reshape(b, s, IN_COLS)
        out_a = _na_attention(proj, na_bias[l]).reshape(b * s, NA_WIDTH)
        out_b = _da_attention(proj, lamv[l], da_bias, subln_g[l].reshape(DA_V_DIM, 1).astype(F32), lam_init)
        out_b = out_b.reshape(b * s, DA_WIDTH)
        if l + 1 < depth:
            x2, proj2 = _outproj_inproj(out_a, out_b, proj2, x2, w_out_b[l], norm_rows[l + 1], w_in_b[l + 1],
                                        in_place=(l > 0))
        else:
            x2 = _outproj_final(out_a, out_b, proj2, x2, w_out_b[l], final_g.reshape(1, d).astype(F32))
    return x2.reshape(b, s, d)
```

```python
import functools
import math

import jax
import jax.numpy as jnp
import numpy as np
from jax import lax
from jax.experimental import pallas as pl
from jax.experimental.pallas import tpu as pltpu

F32 = jnp.float32
BF16 = jnp.bfloat16

GRID_W = 64
NA_HEADS = 8
NA_HEAD_DIM = 64
NA_WIDTH = NA_HEADS * NA_HEAD_DIM
NA_WIN_R = 8
NA_WIN_C = 16
DA_HEADS = 4
DA_HEAD_DIM = 64
DA_V_DIM = 2 * DA_HEAD_DIM
DA_WIDTH = DA_HEADS * DA_V_DIM
MIX_WIDTH = NA_WIDTH + DA_WIDTH
IN_COLS = 4 * NA_WIDTH + 4 * DA_WIDTH
T5_BUCKETS = 32
T5_MAX_EXACT = 8
T5_MAX_DIST = 128
NORM_EPS = 1e-6
SUBLN_EPS = 1e-5

LANES = 128
SUBLANES = 8
NEG_BIG = -1e30
LOG2E = math.log2(math.e)

PROJ_BM = 512
PROJ_BN = 1024
NA_RB = 4
NA_KG = 3
NA_ONES_ROWS = 16
DA_BQ = 256
DA_BK = 256
DA_ONES_ROWS = 16
DA_TILE0 = -(-(T5_MAX_DIST + DA_BK) // DA_BQ)
DA_NTILES = DA_TILE0 + 2 + -(-T5_MAX_DIST // DA_BQ)
DA_NEAR = 3
VMEM_LIMIT = 48 * 1024 * 1024

_QA, _KA, _VA = 0, NA_WIDTH // LANES, 2 * NA_WIDTH // LANES
_QB = 4 * NA_WIDTH // LANES
_KB = _QB + DA_WIDTH // LANES
_VB = _KB + DA_WIDTH // LANES


def _norm_project(x, g_ref, w_ref, o_ref):
    ms = jnp.mean(x * x, axis=-1, keepdims=True)
    h = (x * lax.rsqrt(ms + NORM_EPS) * g_ref[...]).astype(BF16)
    for n in range(IN_COLS // PROJ_BN):
        cols = slice(n * PROJ_BN, (n + 1) * PROJ_BN)
        o_ref[:, cols] = jnp.dot(h, w_ref[:, cols], preferred_element_type=F32).astype(BF16)


def _inproj_kernel(x_ref, g_ref, w_ref, o_ref):
    _norm_project(x_ref[...], g_ref, w_ref, o_ref)


def _inproj(x2, g, w):
    m, d = x2.shape
    return pl.pallas_call(
        _inproj_kernel,
        grid=(m // PROJ_BM,),
        in_specs=[
            pl.BlockSpec((PROJ_BM, d), lambda i: (i, 0)),
            pl.BlockSpec((1, d), lambda i: (0, 0)),
            pl.BlockSpec((d, IN_COLS), lambda i: (0, 0)),
        ],
        out_specs=pl.BlockSpec((PROJ_BM, IN_COLS), lambda i: (i, 0)),
        out_shape=jax.ShapeDtypeStruct((m, IN_COLS), BF16),
        compiler_params=pltpu.CompilerParams(
            dimension_semantics=("arbitrary",), vmem_limit_bytes=VMEM_LIMIT),
        name="inproj",
    )(x2, g, w)


def _fold_logit_scale(w_in):
    col = np.ones((IN_COLS,), np.float32)
    col[0:NA_WIDTH] = NA_HEAD_DIM ** -0.5 * LOG2E
    col[4 * NA_WIDTH:4 * NA_WIDTH + DA_WIDTH] = DA_HEAD_DIM ** -0.5 * LOG2E
    return (w_in * col).astype(BF16)


def _na_block_table(kind):
    delta, rs_rel = {
        "first": (0, lambda x: 0),
        "mid": (-(NA_WIN_R // 2), lambda x: x - NA_WIN_R // 2),
        "last": (-NA_WIN_R, lambda x: NA_RB - NA_WIN_R),
    }[kind]
    table = {}
    for i in range(NA_KG * NA_RB):
        for x in range(NA_RB):
            if rs_rel(x) <= delta + i < rs_rel(x) + NA_WIN_R:
                table[(i, x)] = delta + i - x + (NA_WIN_R - 1)
    return table


def _na_make_block(q_ref, k_ref, v_ref, t_ref, o_ref, vt_scr, rows):
    ngroups = rows // NA_RB
    gk = NA_RB * GRID_W
    lo = lax.broadcasted_iota(jnp.int32, (1, LANES), 1) < NA_HEAD_DIM

    for g in range(ngroups):
        vt_scr[g, :LANES, :] = v_ref[0, g * gk:(g + 1) * gk, :].T
        vt_scr[g, LANES:, :] = jnp.ones((NA_ONES_ROWS, gk), BF16)

    def block(m, g0, kind):
        table = _na_block_table(kind)
        live = [g for g in range(NA_KG) if any((NA_RB * g + ii, x) in table
                                               for ii in range(NA_RB) for x in range(NA_RB))]
        qstart = pl.multiple_of(m * gk, gk)
        q = q_ref[0, pl.ds(qstart, gk), :]
        zero = jnp.zeros((GRID_W, LANES), q.dtype)
        pieces = []
        for x in range(NA_RB):
            qx = q[x * GRID_W:(x + 1) * GRID_W]
            pieces += [jnp.where(lo, qx, zero), jnp.where(lo, zero, qx)]
        qq_t = jnp.concatenate(pieces, axis=0).T
        kw = k_ref[0, pl.ds(pl.multiple_of((g0 + live[0]) * gk, gk), len(live) * gk), :]
        s_all = jnp.dot(kw, qq_t, preferred_element_type=F32)
        s = {g: s_all[n * gk:(n + 1) * gk] for n, g in enumerate(live)}
        p_blocks = {}
        for x in range(NA_RB):
            blks = {}
            for i in range(NA_KG * NA_RB):
                if (i, x) in table:
                    g, ii = divmod(i, NA_RB)
                    sb = s[g][ii * GRID_W:(ii + 1) * GRID_W, x * LANES:(x + 1) * LANES]
                    blks[i] = sb + t_ref[0, table[(i, x)]]
            mx = functools.reduce(jnp.maximum, blks.values())
            mx = jnp.max(mx, axis=0, keepdims=True)
            for i, sb in blks.items():
                p_blocks[(i, x)] = jnp.exp2(sb - mx)
        zero_blk = jnp.zeros((GRID_W, LANES), F32)
        p_rows = [jnp.concatenate([p_blocks.get((NA_RB * g + ii, x), zero_blk) for x in range(NA_RB)], axis=1)
                  for g in live for ii in range(NA_RB)]
        p_all = jnp.concatenate(p_rows, axis=0).astype(BF16)
        vt = jnp.concatenate([vt_scr[g0 + g] for g in live], axis=1)
        o_t = jnp.dot(vt, p_all, preferred_element_type=F32)
        inv_l = 1.0 / o_t[LANES:LANES + 1]
        o = (o_t[:LANES] * inv_l).T
        for x in range(NA_RB):
            top = o[x * LANES:x * LANES + GRID_W]
            bot = o[x * LANES + GRID_W:(x + 1) * LANES]
            o_ref[0, pl.ds(qstart + x * GRID_W, GRID_W), :] = jnp.where(lo, top, bot).astype(BF16)

    return block


def _na_bias_tables(na_rpb):
    nl = na_rpb.shape[0]
    c = np.arange(GRID_W)[:, None]
    j = np.arange(GRID_W)[None, :]
    cs = np.clip(c - NA_WIN_C // 2, 0, GRID_W - NA_WIN_C)
    inside = (j >= cs) & (j < cs + NA_WIN_C)
    col_sel = ((j - c + (NA_WIN_C - 1))[:, :, None] == np.arange(2 * NA_WIN_C - 1)) & inside[:, :, None]
    rpb = (na_rpb * LOG2E).reshape(nl, NA_HEADS // 2, 2, 2 * NA_WIN_R - 1, 2 * NA_WIN_C - 1)
    t = jnp.einsum("lpeab,cjb->lpajec", rpb, col_sel.astype(np.float32), precision=lax.Precision.HIGHEST)
    t = t + np.where(inside, 0.0, NEG_BIG).astype(np.float32).T[:, None, :]
    return t.reshape(nl, NA_HEADS // 2, 2 * NA_WIN_R - 1, GRID_W, LANES)


def _attn_kernel(lamv_ref, q_ref, k_ref, v_ref, u_ref, g_ref, qa_ref, ka_ref, va_ref, ta_ref, o_ref, oa_ref,
                 vt_scr, s_even, s_odd, tiles_scr, vta_scr, *, seq, lam_init):
    na_block = _na_make_block(qa_ref, ka_ref, va_ref, ta_ref, oa_ref, vta_scr, seq // GRID_W)
    na_last = seq // DA_BQ - 1
    nchunks = seq // DA_BK
    nq = seq // DA_BQ
    lo = lax.broadcasted_iota(jnp.int32, (1, LANES), 1) < DA_HEAD_DIM
    nsub = DA_BK // SUBLANES

    lv = lamv_ref[...]
    lam = (jnp.exp(jnp.sum(lv[0:1] * lv[1:2], axis=-1, keepdims=True))
           - jnp.exp(jnp.sum(lv[2:3] * lv[3:4], axis=-1, keepdims=True)) + lam_init)

    for c in range(nchunks):
        vt_scr[c, :DA_V_DIM, :] = v_ref[0, c * DA_BK:(c + 1) * DA_BK, :].T
        vt_scr[c, DA_V_DIM:, :] = jnp.ones((DA_ONES_ROWS, DA_BK), BF16)

    @pl.when(pl.program_id(1) == 0)
    def _():
        for t in range(DA_NTILES):
            rows = jnp.broadcast_to(u_ref[0, t], (DA_BK, DA_BQ + DA_BK))
            tiles_scr[t] = pltpu.roll(rows, 0, 1, stride=1, stride_axis=0)[:, :DA_BQ]

    delta = u_ref[0, DA_NTILES - 1, 0:1, 0:1]
    ratio = DA_BK // DA_BQ

    def step(qi_next, s_next, qi, s_cur, m_cur, qi_done, acc_done, na_work=None):
        if na_work is not None:
            na_work()
        if qi_done is not None:
            a1, a2 = acc_done
            l1 = a1[DA_V_DIM:DA_V_DIM + 1]
            l2 = a2[DA_V_DIM:DA_V_DIM + 1]
            o = a1[:DA_V_DIM] / l1 - lam * (a2[:DA_V_DIM] / l2)
            ms = jnp.mean(o * o, axis=0, keepdims=True)
            y = o * lax.rsqrt(ms + SUBLN_EPS) * g_ref[...]
            y = y * (1.0 - lam_init)
            o_ref[0, pl.ds(pl.multiple_of(qi_done * DA_BQ, DA_BQ), DA_BQ), :] = y.T.astype(BF16)
        if qi_next is not None:
            q = q_ref[0, pl.ds(pl.multiple_of(qi_next * DA_BQ, DA_BQ), DA_BQ), :]
            zero = jnp.zeros_like(q)
            q_t = jnp.concatenate([jnp.where(lo, q, zero), jnp.where(lo, zero, q)], axis=0).T
            own_next = qi_next // ratio
        if qi is not None:
            own = qi // ratio
        m8 = None
        a1 = jnp.zeros((DA_V_DIM + DA_ONES_ROWS, DA_BQ), F32)
        a2 = jnp.zeros((DA_V_DIM + DA_ONES_ROWS, DA_BQ), F32)
        for c in range(nchunks):
            if qi_next is not None:
                cl = lax.rem(own_next + (nchunks - 1) + c, nchunks)
                kc = k_ref[0, pl.ds(pl.multiple_of(cl * DA_BK, DA_BK), DA_BK), :]
                s = jnp.dot(kc, q_t, preferred_element_type=F32)
                if c < DA_NEAR:
                    bt = tiles_scr[jnp.clip(ratio * cl - qi_next + DA_TILE0, 0, DA_NTILES - 1)]
                    s = s + jnp.concatenate([bt, bt], axis=1)
                s_next[cl] = s
                cm = jnp.max(s.reshape(nsub, SUBLANES, 2 * DA_BQ), axis=0)
                if c >= DA_NEAR:
                    cm = cm + jnp.where(own_next + c <= nchunks, delta, 0.0)
                m8 = cm if m8 is None else jnp.maximum(m8, cm)
            if qi is not None:
                far = lax.rem(c - own + 1 + nchunks, nchunks) >= DA_NEAR
                m_c = m_cur - jnp.where(jnp.logical_and(far, c > own), delta, 0.0)
                pb = jnp.exp2(s_cur[c] - m_c).astype(BF16)
                vt = vt_scr[c]
                a1 = a1 + jnp.dot(vt, pb[:, :DA_BQ], preferred_element_type=F32)
                a2 = a2 + jnp.dot(vt, pb[:, DA_BQ:], preferred_element_type=F32)
        return (None if m8 is None else jnp.max(m8, axis=0, keepdims=True)), (a1, a2)

    m0, _ = step(0, s_even, None, None, None, None, None)
    m1, acc0 = step(1, s_odd, 0, s_even, m0, None, None, na_work=lambda: na_block(0, 0, "first"))

    def pair(jj, carry):
        m_prev, acc_prev = carry
        k = 2 * jj
        m_k, acc_km1 = step(k, s_even, k - 1, s_odd, m_prev, k - 2, acc_prev,
                            na_work=lambda: na_block(k - 1, k - 2, "mid"))
        return step(k + 1, s_odd, k, s_even, m_k, k - 1, acc_km1, na_work=lambda: na_block(k, k - 1, "mid"))

    m_last, acc_prev = lax.fori_loop(1, nq // 2, pair, (m1, acc0))
    _, acc_last = step(None, None, nq - 1, s_odd, m_last, nq - 2, acc_prev,
                       na_work=lambda: na_block(na_last, na_last + 1 - NA_KG, "last"))
    step(None, None, None, None, None, nq - 1, acc_last)


def _attention(proj, lamv, bias_rows, subln_col, na_table, lam_init):
    b, s, _ = proj.shape
    rows = s // GRID_W
    assert DA_HEADS == NA_HEADS // 2 and DA_BQ == NA_RB * GRID_W
    assert (s // DA_BQ) % 2 == 0 and s // DA_BK >= DA_NEAR and rows >= NA_KG * NA_RB
    blk = lambda off: pl.BlockSpec((1, s, LANES), lambda h, bi: (bi, 0, off + h))
    logits_scratch = pltpu.VMEM((s // DA_BK, DA_BK, 2 * DA_BQ), F32)
    out_blk = pl.BlockSpec((1, s, LANES), lambda h, bi: (bi, 0, h))
    out_b, out_a = pl.pallas_call(
        functools.partial(_attn_kernel, seq=s, lam_init=lam_init),
        grid=(DA_HEADS, b),
        in_specs=[
            pl.BlockSpec((4, DA_HEAD_DIM), lambda h, bi: (0, 0)),
            blk(_QB), blk(_KB), blk(_VB),
            pl.BlockSpec((1, DA_NTILES, 1, DA_BQ + DA_BK), lambda h, bi: (h, 0, 0, 0)),
            pl.BlockSpec((DA_V_DIM, 1), lambda h, bi: (0, 0)),
            blk(_QA), blk(_KA), blk(_VA),
            pl.BlockSpec((1, 2 * NA_WIN_R - 1, GRID_W, LANES), lambda h, bi: (h, 0, 0, 0)),
        ],
        out_specs=[out_blk, out_blk],
        out_shape=[jax.ShapeDtypeStruct((b, s, DA_WIDTH), BF16), jax.ShapeDtypeStruct((b, s, NA_WIDTH), BF16)],
        scratch_shapes=[
            pltpu.VMEM((s // DA_BK, DA_V_DIM + DA_ONES_ROWS, DA_BK), BF16),
            logits_scratch,
            logits_scratch,
            pltpu.VMEM((DA_NTILES, DA_BK, DA_BQ), F32),
            pltpu.VMEM((rows // NA_RB, LANES + NA_ONES_ROWS, NA_RB * GRID_W), BF16),
        ],
        compiler_params=pltpu.CompilerParams(
            dimension_semantics=("arbitrary", "arbitrary"), vmem_limit_bytes=VMEM_LIMIT),
        name="attn",
    )(lamv, proj, proj, proj, bias_rows, subln_col, proj, proj, proj, na_table)
    return out_a, out_b


def _t5_bucket(rel):
    n = T5_BUCKETS // 2
    ret = jnp.where(rel > 0, n, 0)
    a = jnp.abs(rel)
    small = a < T5_MAX_EXACT
    af = jnp.maximum(a, 1).astype(jnp.float32)
    large = T5_MAX_EXACT + (jnp.log(af / T5_MAX_EXACT) / math.log(T5_MAX_DIST / T5_MAX_EXACT)
                            * (n - T5_MAX_EXACT)).astype(jnp.int32)
    large = jnp.minimum(large, n - 1)
    return ret + jnp.where(small, a, large)


def _da_bias_rows(t5_table):
    period = DA_BQ + DA_BK
    m = np.arange(period)
    rel = np.where(m <= DA_BQ, -m, period - m)
    off = DA_BQ * (np.arange(DA_NTILES) - DA_TILE0)
    u = t5_table[_t5_bucket(jnp.asarray(off[:, None] + rel[None, :], jnp.int32))]
    u = (u * LOG2E).astype(F32).transpose(2, 0, 1)
    u = u - u[:, :1, :1]
    return u.reshape(DA_HEADS, DA_NTILES, 1, period)


def _silu(g):
    return g / (1.0 + jnp.exp(-g))


def _mix_residual(oa_ref, ob_ref, ga_ref, gb_ref, x_ref, w_ref):
    ya = (oa_ref[...].astype(F32) * _silu(ga_ref[...].astype(F32))).astype(BF16)
    yb = (ob_ref[...].astype(F32) * _silu(gb_ref[...].astype(F32))).astype(BF16)
    acc = jnp.dot(ya, w_ref[:NA_WIDTH, :], preferred_element_type=F32)
    acc = acc + jnp.dot(yb, w_ref[NA_WIDTH:, :], preferred_element_type=F32)
    return x_ref[...] + acc


def _outproj_final_kernel(oa_ref, ob_ref, ga_ref, gb_ref, x_ref, w_ref, fg_ref, o_ref):
    xn = _mix_residual(oa_ref, ob_ref, ga_ref, gb_ref, x_ref, w_ref)
    ms = jnp.mean(xn * xn, axis=-1, keepdims=True)
    o_ref[...] = xn * lax.rsqrt(ms + NORM_EPS) * fg_ref[...]


def _outproj_inproj_kernel(oa_ref, ob_ref, ga_ref, gb_ref, x_ref, wo_ref, g_ref, wi_ref, xo_ref, p_ref):
    xn = _mix_residual(oa_ref, ob_ref, ga_ref, gb_ref, x_ref, wo_ref)
    xo_ref[...] = xn
    _norm_project(xn, g_ref, wi_ref, p_ref)


def _mix_in_specs(d):
    ga_blk = 3 * NA_WIDTH // NA_WIDTH
    gb_blk = (4 * NA_WIDTH + 3 * DA_WIDTH) // DA_WIDTH
    return [
        pl.BlockSpec((PROJ_BM, NA_WIDTH), lambda i: (i, 0)),
        pl.BlockSpec((PROJ_BM, DA_WIDTH), lambda i: (i, 0)),
        pl.BlockSpec((PROJ_BM, NA_WIDTH), lambda i: (i, ga_blk)),
        pl.BlockSpec((PROJ_BM, DA_WIDTH), lambda i: (i, gb_blk)),
        pl.BlockSpec((PROJ_BM, d), lambda i: (i, 0)),
        pl.BlockSpec((MIX_WIDTH, d), lambda i: (0, 0), pipeline_mode=pl.Buffered(1)),
    ]


def _outproj_final(out_a, out_b, proj2, x2, w, final_g):
    m, d = x2.shape
    return pl.pallas_call(
        _outproj_final_kernel,
        grid=(m // PROJ_BM,),
        in_specs=_mix_in_specs(d) + [pl.BlockSpec((1, d), lambda i: (0, 0))],
        out_specs=pl.BlockSpec((PROJ_BM, d), lambda i: (i, 0)),
        out_shape=jax.ShapeDtypeStruct((m, d), F32),
        input_output_aliases={4: 0},
        compiler_params=pltpu.CompilerParams(
            dimension_semantics=("arbitrary",), vmem_limit_bytes=VMEM_LIMIT),
        name="outproj_final",
    )(out_a, out_b, proj2, proj2, x2, w, final_g)


def _outproj_inproj(out_a, out_b, proj2, x2, w_out, g, w_in, in_place):
    m, d = x2.shape
    return pl.pallas_call(
        _outproj_inproj_kernel,
        grid=(m // PROJ_BM,),
        in_specs=_mix_in_specs(d) + [
            pl.BlockSpec((1, d), lambda i: (0, 0)),
            pl.BlockSpec((d, IN_COLS), lambda i: (0, 0), pipeline_mode=pl.Buffered(1)),
        ],
        out_specs=[
            pl.BlockSpec((PROJ_BM, d), lambda i: (i, 0)),
            pl.BlockSpec((PROJ_BM, IN_COLS), lambda i: (i, 0)),
        ],
        out_shape=[jax.ShapeDtypeStruct((m, d), F32), jax.ShapeDtypeStruct((m, IN_COLS), BF16)],
        input_output_aliases={4: 0} if in_place else {},
        compiler_params=pltpu.CompilerParams(
            dimension_semantics=("arbitrary",), vmem_limit_bytes=VMEM_LIMIT),
        name="outproj_inproj",
    )(out_a, out_b, proj2, proj2, x2, w_out, g, w_in)


def kernel(x, norm_g, w_in, na_rpb, lambda_q1, lambda_k1, lambda_q2, lambda_k2, subln_g, t5_table, w_out, final_g):
    b, s, d = x.shape
    depth = w_in.shape[0]
    assert s % DA_BK == 0 and s % GRID_W == 0 and (b * s) % PROJ_BM == 0

    w_in_b = _fold_logit_scale(w_in.astype(F32))
    w_out_b = w_out.astype(BF16)
    na_bias = _na_bias_tables(na_rpb.astype(F32))
    da_bias = _da_bias_rows(t5_table.astype(F32))
    lamv = jnp.stack([lambda_q1, lambda_k1, lambda_q2, lambda_k2], axis=1).astype(F32)
    norm_rows = norm_g.reshape(depth, 1, d).astype(F32)

    x2 = x.reshape(b * s, d)
    proj2 = _inproj(x2, norm_rows[0], w_in_b[0])
    for l in range(depth):
        lam_init = 0.8 - 0.6 * math.exp(-0.3 * l)
        proj = proj2.reshape(b, s, IN_COLS)
        out_a, out_b = _attention(proj, lamv[l], da_bias, subln_g[l].reshape(DA_V_DIM, 1).astype(F32), na_bias[l],
                                  lam_init)
        out_a = out_a.reshape(b * s, NA_WIDTH)
        out_b = out_b.reshape(b * s, DA_WIDTH)
        if l + 1 < depth:
            x2, proj2 = _outproj_inproj(out_a, out_b, proj2, x2, w_out_b[l], norm_rows[l + 1], w_in_b[l + 1],
                                        in_place=(l > 0))
        else:
            x2 = _outproj_final(out_a, out_b, proj2, x2, w_out_b[l], final_g.reshape(1, d).astype(F32))
    return x2.reshape(b, s, d)
```

```python
import functools
import math

import jax
import jax.numpy as jnp
import numpy as np
from jax import lax
from jax.experimental import pallas as pl
from jax.experimental.pallas import tpu as pltpu

F32 = jnp.float32
BF16 = jnp.bfloat16

GRID_W = 64
NA_HEADS = 8
NA_HEAD_DIM = 64
NA_WIDTH = NA_HEADS * NA_HEAD_DIM
NA_WIN_R = 8
NA_WIN_C = 16
DA_HEADS = 4
DA_HEAD_DIM = 64
DA_V_DIM = 2 * DA_HEAD_DIM
DA_WIDTH = DA_HEADS * DA_V_DIM
MIX_WIDTH = NA_WIDTH + DA_WIDTH
IN_COLS = 4 * NA_WIDTH + 4 * DA_WIDTH
T5_BUCKETS = 32
T5_MAX_EXACT = 8
T5_MAX_DIST = 128
NORM_EPS = 1e-6
SUBLN_EPS = 1e-5

LANES = 128
SUBLANES = 8
NEG_BIG = -1e30
LOG2E = math.log2(math.e)

PROJ_BM = 512
PROJ_BN = 1024
NA_RB = 4
NA_KG = 3
NA_ONES_ROWS = 16
DA_BQ = 256
DA_BK = 256
DA_ONES_ROWS = 16
DA_TILE0 = -(-(T5_MAX_DIST + DA_BK) // DA_BQ)
DA_NTILES = DA_TILE0 + 2 + -(-T5_MAX_DIST // DA_BQ)
DA_NEAR = 3
VMEM_LIMIT = 48 * 1024 * 1024

_QA, _KA, _VA = 0, NA_WIDTH // LANES, 2 * NA_WIDTH // LANES
_QB = 4 * NA_WIDTH // LANES
_KB = _QB + DA_WIDTH // LANES
_VB = _KB + DA_WIDTH // LANES
_GA = 3 * NA_WIDTH // LANES
_GB = _VB + DA_WIDTH // LANES


def _norm_project(x, g_ref, w_ref, o_ref):
    ms = jnp.mean(x * x, axis=-1, keepdims=True)
    h = (x * lax.rsqrt(ms + NORM_EPS) * g_ref[...]).astype(BF16)
    for n in range(IN_COLS // PROJ_BN):
        cols = slice(n * PROJ_BN, (n + 1) * PROJ_BN)
        o_ref[:, cols] = jnp.dot(h, w_ref[:, cols], preferred_element_type=F32).astype(BF16)


def _inproj_kernel(x_ref, g_ref, w_ref, o_ref):
    _norm_project(x_ref[...], g_ref, w_ref, o_ref)


def _inproj(x2, g, w):
    m, d = x2.shape
    return pl.pallas_call(
        _inproj_kernel,
        grid=(m // PROJ_BM,),
        in_specs=[
            pl.BlockSpec((PROJ_BM, d), lambda i: (i, 0)),
            pl.BlockSpec((1, d), lambda i: (0, 0)),
            pl.BlockSpec((d, IN_COLS), lambda i: (0, 0)),
        ],
        out_specs=pl.BlockSpec((PROJ_BM, IN_COLS), lambda i: (i, 0)),
        out_shape=jax.ShapeDtypeStruct((m, IN_COLS), BF16),
        compiler_params=pltpu.CompilerParams(
            dimension_semantics=("arbitrary",), vmem_limit_bytes=VMEM_LIMIT),
        name="inproj",
    )(x2, g, w)


def _fold_logit_scale(w_in):
    col = np.ones((IN_COLS,), np.float32)
    col[0:NA_WIDTH] = NA_HEAD_DIM ** -0.5 * LOG2E
    col[4 * NA_WIDTH:4 * NA_WIDTH + DA_WIDTH] = DA_HEAD_DIM ** -0.5 * LOG2E
    return (w_in * col).astype(BF16)


def _na_block_table(kind):
    delta, rs_rel = {
        "first": (0, lambda x: 0),
        "mid": (-(NA_WIN_R // 2), lambda x: x - NA_WIN_R // 2),
        "last": (-NA_WIN_R, lambda x: NA_RB - NA_WIN_R),
    }[kind]
    table = {}
    for i in range(NA_KG * NA_RB):
        for x in range(NA_RB):
            if rs_rel(x) <= delta + i < rs_rel(x) + NA_WIN_R:
                table[(i, x)] = delta + i - x + (NA_WIN_R - 1)
    return table


def _na_kernel(q_ref, k_ref, v_ref, gate_ref, t_ref, o_ref, vt_scr, *, rows):
    ngroups = rows // NA_RB
    gk = NA_RB * GRID_W
    lo = lax.broadcasted_iota(jnp.int32, (1, LANES), 1) < NA_HEAD_DIM

    for g in range(ngroups):
        vt_scr[g, :LANES, :] = v_ref[0, g * gk:(g + 1) * gk, :].T
        vt_scr[g, LANES:, :] = jnp.ones((NA_ONES_ROWS, gk), BF16)

    def block(m, g0, kind):
        table = _na_block_table(kind)
        live = [g for g in range(NA_KG) if any((NA_RB * g + ii, x) in table
                                               for ii in range(NA_RB) for x in range(NA_RB))]
        qstart = pl.multiple_of(m * gk, gk)
        q = q_ref[0, pl.ds(qstart, gk), :]
        zero = jnp.zeros((GRID_W, LANES), q.dtype)
        pieces = []
        for x in range(NA_RB):
            qx = q[x * GRID_W:(x + 1) * GRID_W]
            pieces += [jnp.where(lo, qx, zero), jnp.where(lo, zero, qx)]
        qq_t = jnp.concatenate(pieces, axis=0).T
        kw = k_ref[0, pl.ds(pl.multiple_of((g0 + live[0]) * gk, gk), len(live) * gk), :]
        s_all = jnp.dot(kw, qq_t, preferred_element_type=F32)
        s = {g: s_all[n * gk:(n + 1) * gk] for n, g in enumerate(live)}
        p_blocks = {}
        for x in range(NA_RB):
            blks = {}
            for i in range(NA_KG * NA_RB):
                if (i, x) in table:
                    g, ii = divmod(i, NA_RB)
                    sb = s[g][ii * GRID_W:(ii + 1) * GRID_W, x * LANES:(x + 1) * LANES]
                    blks[i] = sb + t_ref[0, table[(i, x)]]
            mx = functools.reduce(jnp.maximum, blks.values())
            mx = jnp.max(mx, axis=0, keepdims=True)
            for i, sb in blks.items():
                p_blocks[(i, x)] = jnp.exp2(sb - mx)
        zero_blk = jnp.zeros((GRID_W, LANES), F32)
        p_rows = [jnp.concatenate([p_blocks.get((NA_RB * g + ii, x), zero_blk) for x in range(NA_RB)], axis=1)
                  for g in live for ii in range(NA_RB)]
        p_all = jnp.concatenate(p_rows, axis=0).astype(BF16)
        vt = jnp.concatenate([vt_scr[g0 + g] for g in live], axis=1)
        o_t = jnp.dot(vt, p_all, preferred_element_type=F32)
        inv_l = 1.0 / o_t[LANES:LANES + 1]
        o = (o_t[:LANES] * inv_l).T
        for x in range(NA_RB):
            top = o[x * LANES:x * LANES + GRID_W]
            bot = o[x * LANES + GRID_W:(x + 1) * LANES]
            rows_x = pl.ds(qstart + x * GRID_W, GRID_W)
            o_ref[0, rows_x, :] = (jnp.where(lo, top, bot) * _silu(gate_ref[0, rows_x, :].astype(F32))).astype(BF16)

    block(0, 0, "first")
    for m in range(1, ngroups - 1):
        block(m, m - 1, "mid")
    block(ngroups - 1, ngroups - NA_KG, "last")


def _na_attention(proj, na_table):
    b, s, _ = proj.shape
    rows = s // GRID_W
    assert rows % NA_RB == 0 and rows >= NA_KG * NA_RB
    npairs = NA_HEADS // 2
    blk = lambda off: pl.BlockSpec((1, s, LANES), lambda hp, bi: (bi, 0, off + hp))
    return pl.pallas_call(
        functools.partial(_na_kernel, rows=rows),
        grid=(npairs, b),
        in_specs=[
            blk(_QA), blk(_KA), blk(_VA), blk(_GA),
            pl.BlockSpec((1, 2 * NA_WIN_R - 1, GRID_W, LANES), lambda hp, bi: (hp, 0, 0, 0)),
        ],
        out_specs=pl.BlockSpec((1, s, LANES), lambda hp, bi: (bi, 0, hp)),
        out_shape=jax.ShapeDtypeStruct((b, s, NA_WIDTH), BF16),
        scratch_shapes=[pltpu.VMEM((rows // NA_RB, LANES + NA_ONES_ROWS, NA_RB * GRID_W), BF16)],
        compiler_params=pltpu.CompilerParams(
            dimension_semantics=("arbitrary", "arbitrary"), vmem_limit_bytes=VMEM_LIMIT),
        name="na_attn",
    )(proj, proj, proj, proj, na_table)


def _na_bias_tables(na_rpb):
    nl = na_rpb.shape[0]
    c = np.arange(GRID_W)[:, None]
    j = np.arange(GRID_W)[None, :]
    cs = np.clip(c - NA_WIN_C // 2, 0, GRID_W - NA_WIN_C)
    inside = (j >= cs) & (j < cs + NA_WIN_C)
    col_sel = ((j - c + (NA_WIN_C - 1))[:, :, None] == np.arange(2 * NA_WIN_C - 1)) & inside[:, :, None]
    rpb = (na_rpb * LOG2E).reshape(nl, NA_HEADS // 2, 2, 2 * NA_WIN_R - 1, 2 * NA_WIN_C - 1)
    t = jnp.einsum("lpeab,cjb->lpajec", rpb, col_sel.astype(np.float32), precision=lax.Precision.HIGHEST)
    t = t + np.where(inside, 0.0, NEG_BIG).astype(np.float32).T[:, None, :]
    return t.reshape(nl, NA_HEADS // 2, 2 * NA_WIN_R - 1, GRID_W, LANES)


def _da_kernel(lamv_ref, q_ref, k_ref, v_ref, gate_ref, u_ref, g_ref, o_ref, vt_scr, s_even, s_odd, tiles_scr,
               *, seq, lam_init):
    nchunks = seq // DA_BK
    nq = seq // DA_BQ
    lo = lax.broadcasted_iota(jnp.int32, (1, LANES), 1) < DA_HEAD_DIM
    nsub = DA_BK // SUBLANES

    lv = lamv_ref[...]
    lam = (jnp.exp(jnp.sum(lv[0:1] * lv[1:2], axis=-1, keepdims=True))
           - jnp.exp(jnp.sum(lv[2:3] * lv[3:4], axis=-1, keepdims=True)) + lam_init)

    for c in range(nchunks):
        vt_scr[c, :DA_V_DIM, :] = v_ref[0, c * DA_BK:(c + 1) * DA_BK, :].T
        vt_scr[c, DA_V_DIM:, :] = jnp.ones((DA_ONES_ROWS, DA_BK), BF16)

    @pl.when(pl.program_id(1) == 0)
    def _():
        for t in range(DA_NTILES):
            rows = jnp.broadcast_to(u_ref[0, t], (DA_BK, DA_BQ + DA_BK))
            tiles_scr[t] = pltpu.roll(rows, 0, 1, stride=1, stride_axis=0)[:, :DA_BQ]

    delta = u_ref[0, DA_NTILES - 1, 0:1, 0:1]
    ratio = DA_BK // DA_BQ

    def step(qi_next, s_next, qi, s_cur, m_cur, qi_done, acc_done):
        if qi_done is not None:
            a1, a2 = acc_done
            l1 = a1[DA_V_DIM:DA_V_DIM + 1]
            l2 = a2[DA_V_DIM:DA_V_DIM + 1]
            o = a1[:DA_V_DIM] / l1 - lam * (a2[:DA_V_DIM] / l2)
            ms = jnp.mean(o * o, axis=0, keepdims=True)
            y = o * lax.rsqrt(ms + SUBLN_EPS) * g_ref[...]
            y = y * (1.0 - lam_init)
            rows_done = pl.ds(pl.multiple_of(qi_done * DA_BQ, DA_BQ), DA_BQ)
            o_ref[0, rows_done, :] = (y.T * _silu(gate_ref[0, rows_done, :].astype(F32))).astype(BF16)
        if qi_next is not None:
            q = q_ref[0, pl.ds(pl.multiple_of(qi_next * DA_BQ, DA_BQ), DA_BQ), :]
            zero = jnp.zeros_like(q)
            q_t = jnp.concatenate([jnp.where(lo, q, zero), jnp.where(lo, zero, q)], axis=0).T
            own_next = qi_next // ratio
        if qi is not None:
            own = qi // ratio
        m8 = None
        a1 = jnp.zeros((DA_V_DIM + DA_ONES_ROWS, DA_BQ), F32)
        a2 = jnp.zeros((DA_V_DIM + DA_ONES_ROWS, DA_BQ), F32)
        for c in range(nchunks):
            if qi_next is not None:
                cl = lax.rem(own_next + (nchunks - 1) + c, nchunks)
                kc = k_ref[0, pl.ds(pl.multiple_of(cl * DA_BK, DA_BK), DA_BK), :]
                s = jnp.dot(kc, q_t, preferred_element_type=F32)
                if c < DA_NEAR:
                    bt = tiles_scr[jnp.clip(ratio * cl - qi_next + DA_TILE0, 0, DA_NTILES - 1)]
                    s = s + jnp.concatenate([bt, bt], axis=1)
                s_next[cl] = s
                cm = jnp.max(s.reshape(nsub, SUBLANES, 2 * DA_BQ), axis=0)
                if c >= DA_NEAR:
                    cm = cm + jnp.where(own_next + c <= nchunks, delta, 0.0)
                m8 = cm if m8 is None else jnp.maximum(m8, cm)
            if qi is not None:
                far = lax.rem(c - own + 1 + nchunks, nchunks) >= DA_NEAR
                m_c = m_cur - jnp.where(jnp.logical_and(far, c > own), delta, 0.0)
                pb = jnp.exp2(s_cur[c] - m_c).astype(BF16)
                vt = vt_scr[c]
                a1 = a1 + jnp.dot(vt, pb[:, :DA_BQ], preferred_element_type=F32)
                a2 = a2 + jnp.dot(vt, pb[:, DA_BQ:], preferred_element_type=F32)
        return (None if m8 is None else jnp.max(m8, axis=0, keepdims=True)), (a1, a2)

    m0, _ = step(0, s_even, None, None, None, None, None)
    m1, acc0 = step(1, s_odd, 0, s_even, m0, None, None)

    def pair(jj, carry):
        m_prev, acc_prev = carry
        k = 2 * jj
        m_k, acc_km1 = step(k, s_even, k - 1, s_odd, m_prev, k - 2, acc_prev)
        return step(k + 1, s_odd, k, s_even, m_k, k - 1, acc_km1)

    m_last, acc_prev = lax.fori_loop(1, nq // 2, pair, (m1, acc0))
    _, acc_last = step(None, None, nq - 1, s_odd, m_last, nq - 2, acc_prev)
    step(None, None, None, None, None, nq - 1, acc_last)


def _da_attention(proj, lamv, bias_rows, subln_col, lam_init):
    b, s, _ = proj.shape
    assert (s // DA_BQ) % 2 == 0 and s // DA_BK >= DA_NEAR
    blk = lambda off: pl.BlockSpec((1, s, LANES), lambda h, bi: (bi, 0, off + h))
    logits_scratch = pltpu.VMEM((s // DA_BK, DA_BK, 2 * DA_BQ), F32)
    return pl.pallas_call(
        functools.partial(_da_kernel, seq=s, lam_init=lam_init),
        grid=(DA_HEADS, b),
        in_specs=[
            pl.BlockSpec((4, DA_HEAD_DIM), lambda h, bi: (0, 0)),
            blk(_QB), blk(_KB), blk(_VB), blk(_GB),
            pl.BlockSpec((1, DA_NTILES, 1, DA_BQ + DA_BK), lambda h, bi: (h, 0, 0, 0)),
            pl.BlockSpec((DA_V_DIM, 1), lambda h, bi: (0, 0)),
        ],
        out_specs=pl.BlockSpec((1, s, LANES), lambda h, bi: (bi, 0, h)),
        out_shape=jax.ShapeDtypeStruct((b, s, DA_WIDTH), BF16),
        scratch_shapes=[
            pltpu.VMEM((s // DA_BK, DA_V_DIM + DA_ONES_ROWS, DA_BK), BF16),
            logits_scratch,
            logits_scratch,
            pltpu.VMEM((DA_NTILES, DA_BK, DA_BQ), F32),
        ],
        compiler_params=pltpu.CompilerParams(
            dimension_semantics=("arbitrary", "arbitrary"), vmem_limit_bytes=VMEM_LIMIT),
        name="da_attn",
    )(lamv, proj, proj, proj, proj, bias_rows, subln_col)


def _t5_bucket(rel):
    n = T5_BUCKETS // 2
    ret = jnp.where(rel > 0, n, 0)
    a = jnp.abs(rel)
    small = a < T5_MAX_EXACT
    af = jnp.maximum(a, 1).astype(jnp.float32)
    large = T5_MAX_EXACT + (jnp.log(af / T5_MAX_EXACT) / math.log(T5_MAX_DIST / T5_MAX_EXACT)
                            * (n - T5_MAX_EXACT)).astype(jnp.int32)
    large = jnp.minimum(large, n - 1)
    return ret + jnp.where(small, a, large)


def _da_bias_rows(t5_table):
    period = DA_BQ + DA_BK
    m = np.arange(period)
    rel = np.where(m <= DA_BQ, -m, period - m)
    off = DA_BQ * (np.arange(DA_NTILES) - DA_TILE0)
    u = t5_table[_t5_bucket(jnp.asarray(off[:, None] + rel[None, :], jnp.int32))]
    u = (u * LOG2E).astype(F32).transpose(2, 0, 1)
    u = u - u[:, :1, :1]
    return u.reshape(DA_HEADS, DA_NTILES, 1, period)


def _silu(g):
    return g / (1.0 + jnp.exp(-g))


def _mix_residual(ya_ref, yb_ref, x_ref, w_ref):
    acc = jnp.dot(ya_ref[...], w_ref[:NA_WIDTH, :], preferred_element_type=F32)
    acc = acc + jnp.dot(yb_ref[...], w_ref[NA_WIDTH:, :], preferred_element_type=F32)
    return x_ref[...] + acc


def _outproj_final_kernel(ya_ref, yb_ref, x_ref, w_ref, fg_ref, o_ref):
    xn = _mix_residual(ya_ref, yb_ref, x_ref, w_ref)
    ms = jnp.mean(xn * xn, axis=-1, keepdims=True)
    o_ref[...] = xn * lax.rsqrt(ms + NORM_EPS) * fg_ref[...]


def _outproj_inproj_kernel(ya_ref, yb_ref, x_ref, wo_ref, g_ref, wi_ref, xo_ref, p_ref):
    xn = _mix_residual(ya_ref, yb_ref, x_ref, wo_ref)
    xo_ref[...] = xn
    _norm_project(xn, g_ref, wi_ref, p_ref)


def _mix_in_specs(d):
    return [
        pl.BlockSpec((PROJ_BM, NA_WIDTH), lambda i: (i, 0)),
        pl.BlockSpec((PROJ_BM, DA_WIDTH), lambda i: (i, 0)),
        pl.BlockSpec((PROJ_BM, d), lambda i: (i, 0)),
        pl.BlockSpec((MIX_WIDTH, d), lambda i: (0, 0), pipeline_mode=pl.Buffered(1)),
    ]


def _outproj_final(y_a, y_b, x2, w, final_g):
    m, d = x2.shape
    return pl.pallas_call(
        _outproj_final_kernel,
        grid=(m // PROJ_BM,),
        in_specs=_mix_in_specs(d) + [pl.BlockSpec((1, d), lambda i: (0, 0))],
        out_specs=pl.BlockSpec((PROJ_BM, d), lambda i: (i, 0)),
        out_shape=jax.ShapeDtypeStruct((m, d), F32),
        input_output_aliases={2: 0},
        compiler_params=pltpu.CompilerParams(
            dimension_semantics=("arbitrary",), vmem_limit_bytes=VMEM_LIMIT),
        name="outproj_final",
    )(y_a, y_b, x2, w, final_g)


def _outproj_inproj(y_a, y_b, x2, w_out, g, w_in, in_place):
    m, d = x2.shape
    return pl.pallas_call(
        _outproj_inproj_kernel,
        grid=(m // PROJ_BM,),
        in_specs=_mix_in_specs(d) + [
            pl.BlockSpec((1, d), lambda i: (0, 0)),
            pl.BlockSpec((d, IN_COLS), lambda i: (0, 0), pipeline_mode=pl.Buffered(1)),
        ],
        out_specs=[
            pl.BlockSpec((PROJ_BM, d), lambda i: (i, 0)),
            pl.BlockSpec((PROJ_BM, IN_COLS), lambda i: (i, 0)),
        ],
        out_shape=[jax.ShapeDtypeStruct((m, d), F32), jax.ShapeDtypeStruct((m, IN_COLS), BF16)],
        input_output_aliases={2: 0} if in_place else {},
        compiler_params=pltpu.CompilerParams(
            dimension_semantics=("arbitrary",), vmem_limit_bytes=VMEM_LIMIT),
        name="outproj_inproj",
    )(y_a, y_b, x2, w_out, g, w_in)


def kernel(x, norm_g, w_in, na_rpb, lambda_q1, lambda_k1, lambda_q2, lambda_k2, subln_g, t5_table, w_out, final_g):
    b, s, d = x.shape
    depth = w_in.shape[0]
    assert s % DA_BK == 0 and s % GRID_W == 0 and (b * s) % PROJ_BM == 0

    w_in_b = _fold_logit_scale(w_in.astype(F32))
    w_out_b = w_out.astype(BF16)
    na_bias = _na_bias_tables(na_rpb.astype(F32))
    da_bias = _da_bias_rows(t5_table.astype(F32))
    lamv = jnp.stack([lambda_q1, lambda_k1, lambda_q2, lambda_k2], axis=1).astype(F32)
    norm_rows = norm_g.reshape(depth, 1, d).astype(F32)

    x2 = x.reshape(b * s, d)
    proj2 = _inproj(x2, norm_rows[0], w_in_b[0])
    for l in range(depth):
        lam_init = 0.8 - 0.6 * math.exp(-0.3 * l)
        proj = proj2.reshape(b, s, IN_COLS)
        out_a = _na_attention(proj, na_bias[l]).reshape(b * s, NA_WIDTH)
        out_b = _da_attention(proj, lamv[l], da_bias, subln_g[l].reshape(DA_V_DIM, 1).astype(F32), lam_init)
        out_b = out_b.reshape(b * s, DA_WIDTH)
        if l + 1 < depth:
            x2, proj2 = _outproj_inproj(out_a, out_b, x2, w_out_b[l], norm_rows[l + 1], w_in_b[l + 1],
                                        in_place=(l > 0))
        else:
            x2 = _outproj_final(out_a, out_b, x2, w_out_b[l], final_g.reshape(1, d).astype(F32))
    return x2.reshape(b, s, d)
```

```python
import functools
import math

import jax
import jax.numpy as jnp
import numpy as np
from jax import lax
from jax.experimental import pallas as pl
from jax.experimental.pallas import tpu as pltpu

F32 = jnp.float32
BF16 = jnp.bfloat16

GRID_W = 64
NA_HEADS = 8
NA_HEAD_DIM = 64
NA_WIDTH = NA_HEADS * NA_HEAD_DIM
NA_WIN_R = 8
NA_WIN_C = 16
DA_HEADS = 4
DA_HEAD_DIM = 64
DA_V_DIM = 2 * DA_HEAD_DIM
DA_WIDTH = DA_HEADS * DA_V_DIM
MIX_WIDTH = NA_WIDTH + DA_WIDTH
IN_COLS = 4 * NA_WIDTH + 4 * DA_WIDTH
T5_BUCKETS = 32
T5_MAX_EXACT = 8
T5_MAX_DIST = 128
NORM_EPS = 1e-6
SUBLN_EPS = 1e-5

LANES = 128
SUBLANES = 8
NEG_BIG = -1e30
LOG2E = math.log2(math.e)

PROJ_BM = 512
PROJ_BN = 1024
FINAL_BM = 1024
NA_RB = 4
NA_KG = 3
NA_ONES_ROWS = 16
DA_BQ = 256
DA_BK = 256
DA_ONES_ROWS = 16
DA_TILE0 = -(-(T5_MAX_DIST + DA_BK) // DA_BQ)
DA_NTILES = DA_TILE0 + 2 + -(-T5_MAX_DIST // DA_BQ)
DA_NEAR = 3
VMEM_LIMIT = 48 * 1024 * 1024

_QA, _KA, _VA = 0, NA_WIDTH // LANES, 2 * NA_WIDTH // LANES
_QB = 4 * NA_WIDTH // LANES
_KB = _QB + DA_WIDTH // LANES
_VB = _KB + DA_WIDTH // LANES
_GA = 3 * NA_WIDTH // LANES
_GB = _VB + DA_WIDTH // LANES


def _norm_project(x, g_ref, w_ref, o_ref):
    ms = jnp.mean(x * x, axis=-1, keepdims=True)
    h = (x * lax.rsqrt(ms + NORM_EPS) * g_ref[...]).astype(BF16)
    for n in range(IN_COLS // PROJ_BN):
        cols = slice(n * PROJ_BN, (n + 1) * PROJ_BN)
        o_ref[:, cols] = jnp.dot(h, w_ref[:, cols], preferred_element_type=F32).astype(BF16)


def _inproj_kernel(x_ref, g_ref, w_ref, o_ref):
    _norm_project(x_ref[...], g_ref, w_ref, o_ref)


def _inproj(x2, g, w):
    m, d = x2.shape
    return pl.pallas_call(
        _inproj_kernel,
        grid=(m // PROJ_BM,),
        in_specs=[
            pl.BlockSpec((PROJ_BM, d), lambda i: (i, 0)),
            pl.BlockSpec((1, d), lambda i: (0, 0)),
            pl.BlockSpec((d, IN_COLS), lambda i: (0, 0)),
        ],
        out_specs=pl.BlockSpec((PROJ_BM, IN_COLS), lambda i: (i, 0)),
        out_shape=jax.ShapeDtypeStruct((m, IN_COLS), BF16),
        compiler_params=pltpu.CompilerParams(
            dimension_semantics=("arbitrary",), vmem_limit_bytes=VMEM_LIMIT),
        name="inproj",
    )(x2, g, w)


def _fold_logit_scale(w_in):
    col = np.ones((IN_COLS,), np.float32)
    col[0:NA_WIDTH] = NA_HEAD_DIM ** -0.5 * LOG2E
    col[4 * NA_WIDTH:4 * NA_WIDTH + DA_WIDTH] = DA_HEAD_DIM ** -0.5 * LOG2E
    return (w_in * col).astype(BF16)


def _na_block_table(kind):
    delta, rs_rel = {
        "first": (0, lambda x: 0),
        "mid": (-(NA_WIN_R // 2), lambda x: x - NA_WIN_R // 2),
        "last": (-NA_WIN_R, lambda x: NA_RB - NA_WIN_R),
    }[kind]
    table = {}
    for i in range(NA_KG * NA_RB):
        for x in range(NA_RB):
            if rs_rel(x) <= delta + i < rs_rel(x) + NA_WIN_R:
                table[(i, x)] = delta + i - x + (NA_WIN_R - 1)
    return table


def _na_kernel(q_ref, k_ref, v_ref, gate_ref, t_ref, o_ref, vt_scr, *, rows):
    ngroups = rows // NA_RB
    gk = NA_RB * GRID_W
    lo = lax.broadcasted_iota(jnp.int32, (1, LANES), 1) < NA_HEAD_DIM

    for g in range(ngroups):
        vt_scr[g, :LANES, :] = v_ref[0, g * gk:(g + 1) * gk, :].T
        vt_scr[g, LANES:, :] = jnp.ones((NA_ONES_ROWS, gk), BF16)

    def block(m, g0, kind):
        table = _na_block_table(kind)
        live = [g for g in range(NA_KG) if any((NA_RB * g + ii, x) in table
                                               for ii in range(NA_RB) for x in range(NA_RB))]
        qstart = pl.multiple_of(m * gk, gk)
        q = q_ref[0, pl.ds(qstart, gk), :]
        zero = jnp.zeros((GRID_W, LANES), q.dtype)
        pieces = []
        for x in range(NA_RB):
            qx = q[x * GRID_W:(x + 1) * GRID_W]
            pieces += [jnp.where(lo, qx, zero), jnp.where(lo, zero, qx)]
        qq_t = jnp.concatenate(pieces, axis=0).T
        kw = k_ref[0, pl.ds(pl.multiple_of((g0 + live[0]) * gk, gk), len(live) * gk), :]
        s_all = jnp.dot(kw, qq_t, preferred_element_type=F32)
        s = {g: s_all[n * gk:(n + 1) * gk] for n, g in enumerate(live)}
        p_blocks = {}
        for x in range(NA_RB):
            blks = {}
            for i in range(NA_KG * NA_RB):
                if (i, x) in table:
                    g, ii = divmod(i, NA_RB)
                    sb = s[g][ii * GRID_W:(ii + 1) * GRID_W, x * LANES:(x + 1) * LANES]
                    blks[i] = sb + t_ref[0, table[(i, x)]]
            mx = functools.reduce(jnp.maximum, blks.values())
            mx = jnp.max(mx, axis=0, keepdims=True)
            for i, sb in blks.items():
                p_blocks[(i, x)] = jnp.exp2(sb - mx)
        zero_blk = jnp.zeros((GRID_W, LANES), F32)
        p_rows = [jnp.concatenate([p_blocks.get((NA_RB * g + ii, x), zero_blk) for x in range(NA_RB)], axis=1)
                  for g in live for ii in range(NA_RB)]
        p_all = jnp.concatenate(p_rows, axis=0).astype(BF16)
        vt = jnp.concatenate([vt_scr[g0 + g] for g in live], axis=1)
        o_t = jnp.dot(vt, p_all, preferred_element_type=F32)
        inv_l = 1.0 / o_t[LANES:LANES + 1]
        o = (o_t[:LANES] * inv_l).T
        for x in range(NA_RB):
            top = o[x * LANES:x * LANES + GRID_W]
            bot = o[x * LANES + GRID_W:(x + 1) * LANES]
            rows_x = pl.ds(qstart + x * GRID_W, GRID_W)
            o_ref[0, rows_x, :] = (jnp.where(lo, top, bot) * _silu(gate_ref[0, rows_x, :].astype(F32))).astype(BF16)

    block(0, 0, "first")
    for m in range(1, ngroups - 1):
        block(m, m - 1, "mid")
    block(ngroups - 1, ngroups - NA_KG, "last")


def _na_attention(proj, na_table):
    b, s, _ = proj.shape
    rows = s // GRID_W
    assert rows % NA_RB == 0 and rows >= NA_KG * NA_RB
    npairs = NA_HEADS // 2
    blk = lambda off: pl.BlockSpec((1, s, LANES), lambda hp, bi: (bi, 0, off + hp))
    return pl.pallas_call(
        functools.partial(_na_kernel, rows=rows),
        grid=(npairs, b),
        in_specs=[
            blk(_QA), blk(_KA), blk(_VA), blk(_GA),
            pl.BlockSpec((1, 2 * NA_WIN_R - 1, GRID_W, LANES), lambda hp, bi: (hp, 0, 0, 0)),
        ],
        out_specs=pl.BlockSpec((1, s, LANES), lambda hp, bi: (bi, 0, hp)),
        out_shape=jax.ShapeDtypeStruct((b, s, NA_WIDTH), BF16),
        scratch_shapes=[pltpu.VMEM((rows // NA_RB, LANES + NA_ONES_ROWS, NA_RB * GRID_W), BF16)],
        compiler_params=pltpu.CompilerParams(
            dimension_semantics=("arbitrary", "arbitrary"), vmem_limit_bytes=VMEM_LIMIT),
        name="na_attn",
    )(proj, proj, proj, proj, na_table)


def _na_bias_tables(na_rpb):
    nl = na_rpb.shape[0]
    c = np.arange(GRID_W)[:, None]
    j = np.arange(GRID_W)[None, :]
    cs = np.clip(c - NA_WIN_C // 2, 0, GRID_W - NA_WIN_C)
    inside = (j >= cs) & (j < cs + NA_WIN_C)
    col_sel = ((j - c + (NA_WIN_C - 1))[:, :, None] == np.arange(2 * NA_WIN_C - 1)) & inside[:, :, None]
    rpb = (na_rpb * LOG2E).reshape(nl, NA_HEADS // 2, 2, 2 * NA_WIN_R - 1, 2 * NA_WIN_C - 1)
    t = jnp.einsum("lpeab,cjb->lpajec", rpb, col_sel.astype(np.float32), precision=lax.Precision.HIGHEST)
    t = t + np.where(inside, 0.0, NEG_BIG).astype(np.float32).T[:, None, :]
    return t.reshape(nl, NA_HEADS // 2, 2 * NA_WIN_R - 1, GRID_W, LANES)


def _da_kernel(lamv_ref, q_ref, k_ref, v_ref, gate_ref, u_ref, g_ref, o_ref, vt_scr, s_even, s_odd, tiles_scr,
               *, seq, lam_init):
    nchunks = seq // DA_BK
    nq = seq // DA_BQ
    lo = lax.broadcasted_iota(jnp.int32, (1, LANES), 1) < DA_HEAD_DIM
    nsub = DA_BK // SUBLANES

    lv = lamv_ref[...]
    lam = (jnp.exp(jnp.sum(lv[0:1] * lv[1:2], axis=-1, keepdims=True))
           - jnp.exp(jnp.sum(lv[2:3] * lv[3:4], axis=-1, keepdims=True)) + lam_init)

    for c in range(nchunks):
        vt_scr[c, :DA_V_DIM, :] = v_ref[0, c * DA_BK:(c + 1) * DA_BK, :].T
        vt_scr[c, DA_V_DIM:, :] = jnp.ones((DA_ONES_ROWS, DA_BK), BF16)

    @pl.when(pl.program_id(1) == 0)
    def _():
        for t in range(DA_NTILES):
            rows = jnp.broadcast_to(u_ref[0, t], (DA_BK, DA_BQ + DA_BK))
            tiles_scr[t] = pltpu.roll(rows, 0, 1, stride=1, stride_axis=0)[:, :DA_BQ]

    delta = u_ref[0, DA_NTILES - 1, 0:1, 0:1]
    ratio = DA_BK // DA_BQ

    def step(qi_next, s_next, qi, s_cur, m_cur, qi_done, acc_done):
        if qi_done is not None:
            a1, a2 = acc_done
            l1 = a1[DA_V_DIM:DA_V_DIM + 1]
            l2 = a2[DA_V_DIM:DA_V_DIM + 1]
            o = a1[:DA_V_DIM] / l1 - lam * (a2[:DA_V_DIM] / l2)
            ms = jnp.mean(o * o, axis=0, keepdims=True)
            y = o * lax.rsqrt(ms + SUBLN_EPS) * g_ref[...]
            y = y * (1.0 - lam_init)
            rows_done = pl.ds(pl.multiple_of(qi_done * DA_BQ, DA_BQ), DA_BQ)
            o_ref[0, rows_done, :] = (y.T * _silu(gate_ref[0, rows_done, :].astype(F32))).astype(BF16)
        if qi_next is not None:
            q = q_ref[0, pl.ds(pl.multiple_of(qi_next * DA_BQ, DA_BQ), DA_BQ), :]
            zero = jnp.zeros_like(q)
            q_t = jnp.concatenate([jnp.where(lo, q, zero), jnp.where(lo, zero, q)], axis=0).T
            own_next = qi_next // ratio
        if qi is not None:
            own = qi // ratio
        m8 = None
        a1 = jnp.zeros((DA_V_DIM + DA_ONES_ROWS, DA_BQ), F32)
        a2 = jnp.zeros((DA_V_DIM + DA_ONES_ROWS, DA_BQ), F32)
        for c in range(nchunks):
            if qi_next is not None:
                cl = lax.rem(own_next + (nchunks - 1) + c, nchunks)
                kc = k_ref[0, pl.ds(pl.multiple_of(cl * DA_BK, DA_BK), DA_BK), :]
                s = jnp.dot(kc, q_t, preferred_element_type=F32)
                if c < DA_NEAR:
                    bt = tiles_scr[jnp.clip(ratio * cl - qi_next + DA_TILE0, 0, DA_NTILES - 1)]
                    s = s + jnp.concatenate([bt, bt], axis=1)
                s_next[cl] = s
                cm = jnp.max(s.reshape(nsub, SUBLANES, 2 * DA_BQ), axis=0)
                if c >= DA_NEAR:
                    cm = cm + jnp.where(own_next + c <= nchunks, delta, 0.0)
                m8 = cm if m8 is None else jnp.maximum(m8, cm)
            if qi is not None:
                far = lax.rem(c - own + 1 + nchunks, nchunks) >= DA_NEAR
                m_c = m_cur - jnp.where(jnp.logical_and(far, c > own), delta, 0.0)
                pb = jnp.exp2(s_cur[c] - m_c).astype(BF16)
                vt = vt_scr[c]
                a1 = a1 + jnp.dot(vt, pb[:, :DA_BQ], preferred_element_type=F32)
                a2 = a2 + jnp.dot(vt, pb[:, DA_BQ:], preferred_element_type=F32)
        return (None if m8 is None else jnp.max(m8, axis=0, keepdims=True)), (a1, a2)

    m0, _ = step(0, s_even, None, None, None, None, None)
    m1, acc0 = step(1, s_odd, 0, s_even, m0, None, None)

    def pair(jj, carry):
        m_prev, acc_prev = carry
        k = 2 * jj
        m_k, acc_km1 = step(k, s_even, k - 1, s_odd, m_prev, k - 2, acc_prev)
        return step(k + 1, s_odd, k, s_even, m_k, k - 1, acc_km1)

    m_last, acc_prev = lax.fori_loop(1, nq // 2, pair, (m1, acc0))
    _, acc_last = step(None, None, nq - 1, s_odd, m_last, nq - 2, acc_prev)
    step(None, None, None, None, None, nq - 1, acc_last)


def _da_attention(proj, lamv, bias_rows, subln_col, lam_init):
    b, s, _ = proj.shape
    assert (s // DA_BQ) % 2 == 0 and s // DA_BK >= DA_NEAR
    blk = lambda off: pl.BlockSpec((1, s, LANES), lambda h, bi: (bi, 0, off + h))
    logits_scratch = pltpu.VMEM((s // DA_BK, DA_BK, 2 * DA_BQ), F32)
    return pl.pallas_call(
        functools.partial(_da_kernel, seq=s, lam_init=lam_init),
        grid=(DA_HEADS, b),
        in_specs=[
            pl.BlockSpec((4, DA_HEAD_DIM), lambda h, bi: (0, 0)),
            blk(_QB), blk(_KB), blk(_VB), blk(_GB),
            pl.BlockSpec((1, DA_NTILES, 1, DA_BQ + DA_BK), lambda h, bi: (h, 0, 0, 0)),
            pl.BlockSpec((DA_V_DIM, 1), lambda h, bi: (0, 0)),
        ],
        out_specs=pl.BlockSpec((1, s, LANES), lambda h, bi: (bi, 0, h)),
        out_shape=jax.ShapeDtypeStruct((b, s, DA_WIDTH), BF16),
        scratch_shapes=[
            pltpu.VMEM((s // DA_BK, DA_V_DIM + DA_ONES_ROWS, DA_BK), BF16),
            logits_scratch,
            logits_scratch,
            pltpu.VMEM((DA_NTILES, DA_BK, DA_BQ), F32),
        ],
        compiler_params=pltpu.CompilerParams(
            dimension_semantics=("arbitrary", "arbitrary"), vmem_limit_bytes=VMEM_LIMIT),
        name="da_attn",
    )(lamv, proj, proj, proj, proj, bias_rows, subln_col)


def _t5_bucket(rel):
    n = T5_BUCKETS // 2
    ret = jnp.where(rel > 0, n, 0)
    a = jnp.abs(rel)
    small = a < T5_MAX_EXACT
    af = jnp.maximum(a, 1).astype(jnp.float32)
    large = T5_MAX_EXACT + (jnp.log(af / T5_MAX_EXACT) / math.log(T5_MAX_DIST / T5_MAX_EXACT)
                            * (n - T5_MAX_EXACT)).astype(jnp.int32)
    large = jnp.minimum(large, n - 1)
    return ret + jnp.where(small, a, large)


def _da_bias_rows(t5_table):
    period = DA_BQ + DA_BK
    m = np.arange(period)
    rel = np.where(m <= DA_BQ, -m, period - m)
    off = DA_BQ * (np.arange(DA_NTILES) - DA_TILE0)
    u = t5_table[_t5_bucket(jnp.asarray(off[:, None] + rel[None, :], jnp.int32))]
    u = (u * LOG2E).astype(F32).transpose(2, 0, 1)
    u = u - u[:, :1, :1]
    return u.reshape(DA_HEADS, DA_NTILES, 1, period)


def _silu(g):
    return g / (1.0 + jnp.exp(-g))


def _mix_residual(ya_ref, yb_ref, x_ref, w_ref):
    acc = jnp.dot(ya_ref[...], w_ref[:NA_WIDTH, :], preferred_element_type=F32)
    acc = acc + jnp.dot(yb_ref[...], w_ref[NA_WIDTH:, :], preferred_element_type=F32)
    return x_ref[...] + acc


def _outproj_final_kernel(ya_ref, yb_ref, x_ref, w_ref, fg_ref, o_ref):
    xn = _mix_residual(ya_ref, yb_ref, x_ref, w_ref)
    ms = jnp.mean(xn * xn, axis=-1, keepdims=True)
    o_ref[...] = xn * lax.rsqrt(ms + NORM_EPS) * fg_ref[...]


def _outproj_inproj_kernel(ya_ref, yb_ref, x_ref, wo_ref, g_ref, wi_ref, xo_ref, p_ref):
    xn = _mix_residual(ya_ref, yb_ref, x_ref, wo_ref)
    xo_ref[...] = xn
    _norm_project(xn, g_ref, wi_ref, p_ref)


def _mix_in_specs(d, bm):
    return [
        pl.BlockSpec((bm, NA_WIDTH), lambda i: (i, 0)),
        pl.BlockSpec((bm, DA_WIDTH), lambda i: (i, 0)),
        pl.BlockSpec((bm, d), lambda i: (i, 0)),
        pl.BlockSpec((MIX_WIDTH, d), lambda i: (0, 0), pipeline_mode=pl.Buffered(1)),
    ]


def _outproj_final(y_a, y_b, x2, w, final_g):
    m, d = x2.shape
    return pl.pallas_call(
        _outproj_final_kernel,
        grid=(m // FINAL_BM,),
        in_specs=_mix_in_specs(d, FINAL_BM) + [pl.BlockSpec((1, d), lambda i: (0, 0))],
        out_specs=pl.BlockSpec((FINAL_BM, d), lambda i: (i, 0)),
        out_shape=jax.ShapeDtypeStruct((m, d), F32),
        input_output_aliases={2: 0},
        compiler_params=pltpu.CompilerParams(
            dimension_semantics=("arbitrary",), vmem_limit_bytes=VMEM_LIMIT),
        name="outproj_final",
    )(y_a, y_b, x2, w, final_g)


def _outproj_inproj(y_a, y_b, x2, w_out, g, w_in, in_place):
    m, d = x2.shape
    return pl.pallas_call(
        _outproj_inproj_kernel,
        grid=(m // PROJ_BM,),
        in_specs=_mix_in_specs(d, PROJ_BM) + [
            pl.BlockSpec((1, d), lambda i: (0, 0)),
            pl.BlockSpec((d, IN_COLS), lambda i: (0, 0), pipeline_mode=pl.Buffered(1)),
        ],
        out_specs=[
            pl.BlockSpec((PROJ_BM, d), lambda i: (i, 0)),
            pl.BlockSpec((PROJ_BM, IN_COLS), lambda i: (i, 0)),
        ],
        out_shape=[jax.ShapeDtypeStruct((m, d), F32), jax.ShapeDtypeStruct((m, IN_COLS), BF16)],
        input_output_aliases={2: 0} if in_place else {},
        compiler_params=pltpu.CompilerParams(
            dimension_semantics=("arbitrary",), vmem_limit_bytes=VMEM_LIMIT),
        name="outproj_inproj",
    )(y_a, y_b, x2, w_out, g, w_in)


def kernel(x, norm_g, w_in, na_rpb, lambda_q1, lambda_k1, lambda_q2, lambda_k2, subln_g, t5_table, w_out, final_g):
    b, s, d = x.shape
    depth = w_in.shape[0]
    assert s % DA_BK == 0 and s % GRID_W == 0 and (b * s) % PROJ_BM == 0 and (b * s) % FINAL_BM == 0

    w_in_b = _fold_logit_scale(w_in.astype(F32))
    w_out_b = w_out.astype(BF16)
    na_bias = _na_bias_tables(na_rpb.astype(F32))
    da_bias = _da_bias_rows(t5_table.astype(F32))
    lamv = jnp.stack([lambda_q1, lambda_k1, lambda_q2, lambda_k2], axis=1).astype(F32)
    norm_rows = norm_g.reshape(depth, 1, d).astype(F32)

    x2 = x.reshape(b * s, d)
    proj2 = _inproj(x2, norm_rows[0], w_in_b[0])
    for l in range(depth):
        lam_init = 0.8 - 0.6 * math.exp(-0.3 * l)
        proj = proj2.reshape(b, s, IN_COLS)
        out_a = _na_attention(proj, na_bias[l]).reshape(b * s, NA_WIDTH)
        out_b = _da_attention(proj, lamv[l], da_bias, subln_g[l].reshape(DA_V_DIM, 1).astype(F32), lam_init)
        out_b = out_b.reshape(b * s, DA_WIDTH)
        if l + 1 < depth:
            x2, proj2 = _outproj_inproj(out_a, out_b, x2, w_out_b[l], norm_rows[l + 1], w_in_b[l + 1],
                                        in_place=(l > 0))
        else:
            x2 = _outproj_final(out_a, out_b, x2, w_out_b[l], final_g.reshape(1, d).astype(F32))
    return x2.reshape(b, s, d)
```
